```python
import math
import jax
import jax.numpy as jnp
from jax import lax
import numpy as np

D_MODEL = 2048
BATCH = 4
SEQ = 2048
DEPTH = 4
DEC_BATCH = 128
DEC_SEQ = 8
PAST_LEN = 8192
PAGE_SIZE = 128

GROUP_W = D_MODEL // 4
MIX_W = 4 * GROUP_W
D_FF = 4 * D_MODEL
ROPE_THETA = 500000.0
EPS = 1e-6
QBLOCK = 128
D_A = 32
H_A = GROUP_W // (2 * D_A)
ROT_A = D_A // 4
V_B = 128
H_B = GROUP_W // V_B
NOPE_B = 64
ROPE_B = 32
Q_LORA = 256
KV_LORA = 128
D_C = 64
H_C = GROUP_W // D_C
ROT_C = D_C // 4
H_IDX = 8
D_IDX = 32
ROT_IDX = D_IDX // 4
DSA_TOPK = 256
D_D = 64
H_D = GROUP_W // D_D
ROT_D = D_D // 4
MOBA_BLOCK = 256
MOBA_TOPK = 3
ROW_A = 4 * D_A
ROW_B = KV_LORA + ROPE_B
ROW_C = 2 * D_C + D_IDX
ROW_D = 2 * D_D
IN_SPLITS = (H_A * 2 * D_A, 2 * D_A, 2 * D_A,
             Q_LORA, KV_LORA, ROPE_B,
             H_C * D_C, D_C, D_C, H_IDX * D_IDX, D_IDX, H_IDX,
             H_D * D_D, D_D, D_D)
IN_COLS = sum(IN_SPLITS)

kernel_name = 'hybrid_diff_mla_dsa_moba_decode_step'


def _rmsnorm(x, g):
    xf = x.astype(jnp.float32)
    y = xf * lax.rsqrt(jnp.mean(xf * xf, axis=-1, keepdims=True) + EPS)
    return y.astype(x.dtype) * g


def _rope(x, pos, n_rot):
    half = n_rot // 2
    inv = jnp.exp(-math.log(ROPE_THETA) * jnp.arange(half, dtype=jnp.float32) / half)
    ang = pos.astype(jnp.float32)[:, None] * inv
    ang = ang.reshape((pos.shape[0],) + (1,) * (x.ndim - 3) + (half,))
    cos = jnp.cos(ang).astype(x.dtype)
    sin = jnp.sin(ang).astype(x.dtype)
    x1 = x[..., :half]
    x2 = x[..., half:n_rot]
    return jnp.concatenate([x1 * cos - x2 * sin, x1 * sin + x2 * cos, x[..., n_rot:]], axis=-1)


def _project(hn, pos, w_in_l, q_norm_l, kv_norm_l, w_uq_l, w_ukv_l, idx_k_norm_l):
    B, T, _ = hn.shape
    cuts = [int(c) for c in np.cumsum(IN_SPLITS)[:-1]]
    (qa, ka, va, cq, ckv, kr, qc, kc, vc, iq, ik, iw, qd, kd, vd) = jnp.split(hn @ w_in_l, cuts, axis=-1)
    qa = _rope(qa.reshape(B, T, H_A, 2, D_A), pos, ROT_A)
    ka = _rope(ka.reshape(B, T, 2, D_A), pos, ROT_A)
    qbh = (_rmsnorm(cq, q_norm_l) @ w_uq_l).reshape(B, T, H_B, NOPE_B + ROPE_B)
    q_nope = qbh[..., :NOPE_B]
    q_rope = _rope(qbh[..., NOPE_B:], pos, ROPE_B)
    ckv = _rmsnorm(ckv, kv_norm_l)
    kr = _rope(kr, pos, ROPE_B)
    q_lat = jnp.einsum('bthn,chn->bthc', q_nope, w_ukv_l[..., :NOPE_B])
    qc = _rope(qc.reshape(B, T, H_C, D_C), pos, ROT_C)
    kc = _rope(kc, pos, ROT_C)
    iq = _rope(iq.reshape(B, T, H_IDX, D_IDX), pos, ROT_IDX)
    ik = _rope(_rmsnorm(ik, idx_k_norm_l), pos, ROT_IDX)
    iw = iw * H_IDX ** -0.5
    qd = _rope(qd.reshape(B, T, H_D, D_D), pos, ROT_D)
    kd = _rope(kd, pos, ROT_D)
    queries = (qa, q_lat, q_rope, qc, iq, iw, qd)
    rows = (jnp.concatenate([ka.reshape(B, T, 2 * D_A), va], axis=-1),
            jnp.concatenate([ckv, kr], axis=-1),
            jnp.concatenate([kc, vc, ik], axis=-1),
            jnp.concatenate([kd, vd], axis=-1))
    return queries, rows


def _diff_attention(q, rows, qpos, lam, lam_init, subln_g):
    L = rows.shape[0]
    k = rows[:, :2 * D_A].reshape(L, 2, D_A)
    v = rows[:, 2 * D_A:]
    s = jnp.einsum('qhmd,kmd->hmqk', q, k).astype(jnp.float32) * D_A ** -0.5
    visible = jnp.arange(L)[None, :] <= qpos[:, None]
    p = jax.nn.softmax(jnp.where(visible, s, -jnp.inf), axis=-1)
    p = p[:, 0] - lam * p[:, 1]
    o = jnp.einsum('hqk,kd->qhd', p.astype(v.dtype), v)
    o = _rmsnorm(o, subln_g) * (1.0 - lam_init)
    return o.reshape(o.shape[0], -1)


def _mla_attention(q_lat, q_rope, rows, qpos, w_uv):
    L = rows.shape[0]
    ckv = rows[:, :KV_LORA]
    kr = rows[:, KV_LORA:]
    s = (jnp.einsum('qhc,kc->hqk', q_lat, ckv) + jnp.einsum('qhr,kr->hqk', q_rope, kr)).astype(jnp.float32)
    s = s * (NOPE_B + ROPE_B) ** -0.5
    visible = jnp.arange(L)[None, :] <= qpos[:, None]
    p = jax.nn.softmax(jnp.where(visible, s, -jnp.inf), axis=-1)
    o_lat = jnp.einsum('hqk,kc->qhc', p.astype(ckv.dtype), ckv)
    o = jnp.einsum('qhc,chv->qhv', o_lat, w_uv)
    return o.reshape(o.shape[0], -1)


def _dsa_attention(q, iq, iw, rows, qpos):
    L = rows.shape[0]
    k = rows[:, :D_C]
    v = rows[:, D_C:2 * D_C]
    ik = rows[:, 2 * D_C:]
    n_keep = min(DSA_TOPK, L // 4)
    rel = jax.nn.relu(jnp.einsum('qhd,kd->qhk', iq, ik).astype(jnp.float32) * D_IDX ** -0.5)
    score = jnp.einsum('qh,qhk->qk', iw.astype(jnp.float32), rel)
    visible = jnp.arange(L)[None, :] <= qpos[:, None]
    _, sel = lax.top_k(jnp.where(visible, score, -jnp.inf), n_keep)
    sel_ok = sel <= qpos[:, None]
    kg = k[sel]
    vg = v[sel]
    s = jnp.einsum('qhd,qkd->qhk', q, kg).astype(jnp.float32) * D_C ** -0.5
    p = jax.nn.softmax(jnp.where(sel_ok[:, None, :], s, -jnp.inf), axis=-1)
    o = jnp.einsum('qhk,qkd->qhd', p.astype(vg.dtype), vg)
    return o.reshape(o.shape[0], -1)


def _moba_attention(q, rows, qpos):
    QB = q.shape[0]
    L = rows.shape[0]
    nb = -(-L // MOBA_BLOCK)
    rows_p = jnp.pad(rows, ((0, nb * MOBA_BLOCK - L), (0, 0)))
    k = rows_p[:, :D_D].reshape(nb, MOBA_BLOCK, D_D)
    v = rows_p[:, D_D:].reshape(nb, MOBA_BLOCK, D_D)
    kmean = jnp.mean(k.astype(jnp.float32), axis=1)
    own = qpos // MOBA_BLOCK
    gate = jnp.einsum('qhd,nd->qhn', q.astype(jnp.float32), kmean)
    gate = jnp.where(jnp.arange(nb)[None, None, :] < own[:, None, None], gate, -jnp.inf)
    n_sel = min(MOBA_TOPK, nb)
    _, sel = lax.top_k(gate, n_sel)
    sel_ok = sel < own[:, None, None]
    ks = k[sel]
    vs = v[sel]
    scale = D_D ** -0.5
    s_sel = jnp.einsum('qhd,qhnkd->qhnk', q, ks).astype(jnp.float32) * scale
    s_sel = jnp.where(sel_ok[..., None], s_sel, -jnp.inf).reshape(QB, H_D, n_sel * MOBA_BLOCK)
    ko = k[own]
    vo = v[own]
    s_own = jnp.einsum('qhd,qkd->qhk', q, ko).astype(jnp.float32) * scale
    kpos_own = own[:, None] * MOBA_BLOCK + jnp.arange(MOBA_BLOCK)[None, :]
    s_own = jnp.where((kpos_own <= qpos[:, None])[:, None, :], s_own, -jnp.inf)
    p = jax.nn.softmax(jnp.concatenate([s_sel, s_own], axis=-1), axis=-1).astype(v.dtype)
    n_s = n_sel * MOBA_BLOCK
    o = (jnp.einsum('qhk,qhkd->qhd', p[..., :n_s], vs.reshape(QB, H_D, n_s, D_D))
         + jnp.einsum('qhk,qkd->qhd', p[..., n_s:], vo))
    return o.reshape(QB, -1)


def _attend(queries, key_fn, B, T, L, lam, lam_init, subln_g, w_uv):
    qa, q_lat, q_rope, qc, iq, iw, qd = queries
    qb = min(QBLOCK, T)
    nqb = T // qb
    pos0 = L - T

    def step(i):
        b = i // nqb
        q0 = (i % nqb) * qb

        def blk(a):
            return lax.dynamic_slice_in_dim(a[b], q0, qb, axis=0)

        qpos = pos0 + q0 + jnp.arange(qb)
        ra, rb, rc, rd = key_fn(b)
        oa = _diff_attention(blk(qa), ra, qpos, lam, lam_init, subln_g)
        ob = _mla_attention(blk(q_lat), blk(q_rope), rb, qpos, w_uv)
        oc = _dsa_attention(blk(qc), blk(iq), blk(iw), rc, qpos)
        od = _moba_attention(blk(qd), rd, qpos)
        return jnp.concatenate([oa, ob, oc, od], axis=-1)

    out = lax.map(step, jnp.arange(B * nqb))
    return out.reshape(B, T, MIX_W)


def _prompt_keys(rows):
    def fn(b):
        return tuple(r[b] for r in rows)
    return fn


def _paged_keys(caches, l, page_table):
    def make(rows):
        def fn(b):
            out = []
            for r, c in zip(rows, caches):
                past = c[l, page_table[b]]
                out.append(jnp.concatenate([past.reshape(-1, c.shape[-1]), r[b]], axis=0))
            return tuple(out)
        return fn
    return make


def _layer(h, pos0, make_keys, lam, lam_init, norm_mix_l, norm_mlp_l, w_in_l, w_out_l, subln_l,
           q_norm_l, kv_norm_l, w_uq_l, w_ukv_l, idx_k_norm_l, w_up_l, w_down_l):
    B, T, _ = h.shape
    pos = pos0 + jnp.arange(T)
    queries, rows = _project(_rmsnorm(h, norm_mix_l), pos, w_in_l, q_norm_l, kv_norm_l,
                             w_uq_l, w_ukv_l, idx_k_norm_l)
    mixed = _attend(queries, make_keys(rows), B, T, pos0 + T, lam, lam_init, subln_l,
                    w_ukv_l[..., NOPE_B:])
    h = h + mixed @ w_out_l
    u = jax.nn.relu(_rmsnorm(h, norm_mlp_l) @ w_up_l)
    h = h + (u * u) @ w_down_l
    return h, rows


def setup_inputs(seed: int = 0) -> dict:
    key = jax.random.key(seed)
    ks = jax.random.split(key, 21)
    f32 = jnp.float32
    n_pages = PAST_LEN // PAGE_SIZE
    n_used = DEC_BATCH * n_pages
    n_pool = (n_used * 5 + 3) // 4

    def nrm(k, shape, scale):
        return jax.random.normal(k, shape, f32) * scale

    def gain(k, shape):
        return 1.0 + 0.05 * jax.random.normal(k, shape, f32)

    page_table = jax.random.permutation(ks[6], n_pool)[:n_used].reshape(DEC_BATCH, n_pages).astype(jnp.int32)
    return {
        'x_prompt': nrm(ks[0], (BATCH, SEQ, D_MODEL), 1.0),
        'x_sample': nrm(ks[1], (DEC_BATCH, DEC_SEQ, D_MODEL), 1.0),
        'cache_diff': nrm(ks[2], (DEPTH, n_pool, PAGE_SIZE, ROW_A), 1.0),
        'cache_mla': nrm(ks[3], (DEPTH, n_pool, PAGE_SIZE, ROW_B), 1.0),
        'cache_dsa': nrm(ks[4], (DEPTH, n_pool, PAGE_SIZE, ROW_C), 1.0),
        'cache_moba': nrm(ks[5], (DEPTH, n_pool, PAGE_SIZE, ROW_D), 1.0),
        'page_table': page_table,
        'norm_mix': gain(ks[7], (DEPTH, D_MODEL)),
        'norm_mlp': gain(ks[8], (DEPTH, D_MODEL)),
        'norm_final': gain(ks[9], (D_MODEL,)),
        'w_in': nrm(ks[10], (DEPTH, D_MODEL, IN_COLS), D_MODEL ** -0.5),
        'w_out': nrm(ks[11], (DEPTH, MIX_W, D_MODEL), MIX_W ** -0.5),
        'diff_lambda': nrm(ks[12], (DEPTH, 4, D_A), 0.1),
        'diff_subln': gain(ks[13], (DEPTH, 2 * D_A)),
        'mla_q_norm': gain(ks[14], (DEPTH, Q_LORA)),
        'mla_kv_norm': gain(ks[15], (DEPTH, KV_LORA)),
        'mla_w_uq': nrm(ks[16], (DEPTH, Q_LORA, H_B * (NOPE_B + ROPE_B)), Q_LORA ** -0.5),
        'mla_w_ukv': nrm(ks[17], (DEPTH, KV_LORA, H_B, NOPE_B + V_B), KV_LORA ** -0.5),
        'dsa_idx_k_norm': gain(ks[18], (DEPTH, D_IDX)),
        'w_up': nrm(ks[19], (DEPTH, D_MODEL, D_FF), D_MODEL ** -0.5),
        'w_down': nrm(ks[20], (DEPTH, D_FF, D_MODEL), D_FF ** -0.5),
    }


def reference(x_prompt, x_sample, cache_diff, cache_mla, cache_dsa, cache_moba, page_table,
              norm_mix, norm_mlp, norm_final, w_in, w_out, diff_lambda, diff_subln,
              mla_q_norm, mla_kv_norm, mla_w_uq, mla_w_ukv, dsa_idx_k_norm, w_up, w_down):
    caches = (cache_diff, cache_mla, cache_dsa, cache_moba)
    past_len = page_table.shape[1] * cache_diff.shape[2]
    h_p = x_prompt
    h_s = x_sample
    rows_p = ([], [], [], [])
    rows_s = ([], [], [], [])
    for l in range(DEPTH):
        lam_init = 0.8 - 0.6 * math.exp(-0.3 * l)
        lv = diff_lambda[l].astype(jnp.float32)
        lam = jnp.exp(jnp.sum(lv[0] * lv[1])) - jnp.exp(jnp.sum(lv[2] * lv[3])) + lam_init
        wl = (norm_mix[l], norm_mlp[l], w_in[l], w_out[l], diff_subln[l], mla_q_norm[l], mla_kv_norm[l],
              mla_w_uq[l], mla_w_ukv[l], dsa_idx_k_norm[l], w_up[l], w_down[l])
        h_p, r_p = _layer(h_p, 0, _prompt_keys, lam, lam_init, *wl)
        h_s, r_s = _layer(h_s, past_len, _paged_keys(caches, l, page_table), lam, lam_init, *wl)
        for j in range(4):
            rows_p[j].append(r_p[j])
            rows_s[j].append(r_s[j])
    y_prompt = _rmsnorm(h_p, norm_final)
    y_sample = _rmsnorm(h_s, norm_final)
    return (y_prompt, y_sample,
            jnp.stack(rows_p[0]), jnp.stack(rows_s[0]),
            jnp.stack(rows_p[1]), jnp.stack(rows_s[1]),
            jnp.stack(rows_p[2]), jnp.stack(rows_s[2]),
            jnp.stack(rows_p[3]), jnp.stack(rows_s[3]))
```

```python
import functools
import math

import numpy as np
import jax
import jax.numpy as jnp
from jax import lax
from jax.experimental import pallas as pl
from jax.experimental.pallas import tpu as pltpu

F32 = jnp.float32
BF16 = jnp.bfloat16
NEG_INF = float("-inf")
INT_MIN = -(2 ** 31)

D_MODEL = 2048
GROUP_W = D_MODEL // 4
D_FF = 4 * D_MODEL
ROPE_THETA = 500000.0
EPS = 1e-6
D_A = 32
H_A = GROUP_W // (2 * D_A)
ROT_A = D_A // 4
V_B = 128
H_B = GROUP_W // V_B
NOPE_B = 64
ROPE_B = 32
Q_LORA = 256
KV_LORA = 128
D_C = 64
H_C = GROUP_W // D_C
ROT_C = D_C // 4
H_IDX = 8
D_IDX = 32
ROT_IDX = D_IDX // 4
DSA_TOPK = 256
D_D = 64
H_D = GROUP_W // D_D
ROT_D = D_D // 4
MOBA_BLOCK = 256
MOBA_SHIFT = 8
MOBA_TOPK = 3
ROW_A = 4 * D_A
ROW_B = KV_LORA + ROPE_B
ROW_C = 2 * D_C + D_IDX
ROW_D = 2 * D_D

LANES = 128

OFF_QA = 0
OFF_QC = 512
OFF_QD = 1024
OFF_IQ = 1536
OFF_CQ = 1792
OFF_RA = 2048
OFF_CKV = 2176
OFF_MISC = 2304
OFF_RC = 2432
OFF_RD = 2560
IN_COLS_P = 2688

T_A, T_C, T_RA, T_RC, T_B, T_M = 0, 3, 6, 9, 12, 15
N_TAB = 20

VMEM_LIMIT = 56 * 1024 * 1024

TM_PROJ = 512
TN_PROJ = 896
TM_POST = 256
TM_MLP = 512
TF_MLP = 512
TQ = 128


def _cparams(sem):
    return pltpu.CompilerParams(dimension_semantics=sem, vmem_limit_bytes=VMEM_LIMIT)


def _rms(x, g):
    return x * lax.rsqrt(jnp.mean(x * x, axis=-1, keepdims=True) + EPS) * g


def _lane(shape):
    return lax.broadcasted_iota(jnp.int32, shape, len(shape) - 1)


def _row(shape):
    return lax.broadcasted_iota(jnp.int32, shape, 0)


def _dot_nt(a, b):
    return lax.dot_general(a, b, (((1,), (1,)), ((), ())), preferred_element_type=F32)


def _dot(a, b):
    return jnp.dot(a, b, preferred_element_type=F32)


def _norm_matmul_kernel(x_ref, g_ref, w_ref, o_ref, xn_ref):
    @pl.when(pl.program_id(1) == 0)
    def _():
        xn_ref[...] = _rms(x_ref[...], g_ref[...]).astype(BF16)

    o_ref[...] = _dot(xn_ref[...], w_ref[...])


def _norm_matmul(x, g, w):
    n, d = x.shape
    cols = w.shape[1]
    return pl.pallas_call(
        _norm_matmul_kernel,
        out_shape=jax.ShapeDtypeStruct((n, cols), F32),
        grid=(n // TM_PROJ, cols // TN_PROJ),
        in_specs=[pl.BlockSpec((TM_PROJ, d), lambda i, j: (i, 0)),
                  pl.BlockSpec((1, d), lambda i, j: (0, 0)),
                  pl.BlockSpec((d, TN_PROJ), lambda i, j: (0, j))],
        out_specs=pl.BlockSpec((TM_PROJ, TN_PROJ), lambda i, j: (i, j)),
        scratch_shapes=[pltpu.VMEM((TM_PROJ, d), BF16)],
        compiler_params=_cparams(("arbitrary", "arbitrary")),
        name="norm_in_proj",
    )(x, g, w)


def _matmul_res_kernel(x_ref, w_ref, r_ref, o_ref):
    o_ref[...] = r_ref[...] + _dot(x_ref[...], w_ref[...])


def _matmul_res(x16, w16, res):
    n, k = x16.shape
    cols = w16.shape[1]
    tn = 1024
    return pl.pallas_call(
        _matmul_res_kernel,
        out_shape=jax.ShapeDtypeStruct((n, cols), F32),
        grid=(n // TM_PROJ, cols // tn),
        in_specs=[pl.BlockSpec((TM_PROJ, k), lambda i, j: (i, 0)),
                  pl.BlockSpec((k, tn), lambda i, j: (0, j)),
                  pl.BlockSpec((TM_PROJ, tn), lambda i, j: (i, j))],
        out_specs=pl.BlockSpec((TM_PROJ, tn), lambda i, j: (i, j)),
        compiler_params=_cparams(("arbitrary", "arbitrary")),
        name="out_proj",
    )(x16, w16, res)


def _mlp_kernel(h_ref, g_ref, wu_ref, wd_ref, o_ref, xn_ref, acc_ref):
    f = pl.program_id(1)

    @pl.when(f == 0)
    def _():
        xn_ref[...] = _rms(h_ref[...], g_ref[...]).astype(BF16)
        acc_ref[...] = h_ref[...]

    u = jnp.maximum(_dot(xn_ref[...], wu_ref[...]), 0.0)
    acc_ref[...] += _dot((u * u).astype(BF16), wd_ref[...])

    @pl.when(f == pl.num_programs(1) - 1)
    def _():
        o_ref[...] = acc_ref[...]


def _mlp(h, g, wu16, wd16):
    n, d = h.shape
    ff = wu16.shape[1]
    return pl.pallas_call(
        _mlp_kernel,
        out_shape=jax.ShapeDtypeStruct((n, d), F32),
        grid=(n // TM_MLP, ff // TF_MLP),
        in_specs=[pl.BlockSpec((TM_MLP, d), lambda i, f: (i, 0)),
                  pl.BlockSpec((1, d), lambda i, f: (0, 0)),
                  pl.BlockSpec((d, TF_MLP), lambda i, f: (0, f)),
                  pl.BlockSpec((TF_MLP, d), lambda i, f: (f, 0))],
        out_specs=pl.BlockSpec((TM_MLP, d), lambda i, f: (i, 0)),
        scratch_shapes=[pltpu.VMEM((TM_MLP, d), BF16), pltpu.VMEM((TM_MLP, d), F32)],
        compiler_params=_cparams(("arbitrary", "arbitrary")),
        name="mlp",
    )(h, g, wu16, wd16)


def _final_norm_kernel(x_ref, g_ref, o_ref):
    o_ref[...] = _rms(x_ref[...], g_ref[...])


def _final_norm(x, g):
    n, d = x.shape
    return pl.pallas_call(
        _final_norm_kernel,
        out_shape=jax.ShapeDtypeStruct((n, d), F32),
        grid=(n // TM_PROJ,),
        in_specs=[pl.BlockSpec((TM_PROJ, d), lambda i: (i, 0)),
                  pl.BlockSpec((1, d), lambda i: (0, 0))],
        out_specs=pl.BlockSpec((TM_PROJ, d), lambda i: (i, 0)),
        compiler_params=_cparams(("arbitrary",)),
        name="final_norm",
    )(x, g)


def _rope(x, tab_ref, slot, half):
    c = tab_ref[:, (slot + 0) * LANES:(slot + 1) * LANES]
    s1 = tab_ref[:, (slot + 1) * LANES:(slot + 2) * LANES]
    s2 = tab_ref[:, (slot + 2) * LANES:(slot + 3) * LANES]
    return x * c + pltpu.roll(x, LANES - half, 1) * s1 + pltpu.roll(x, half, 1) * s2


def _post_kernel(p_ref, tab_ref, qn_ref, kvn_ref, ikn_ref, wuq_ref, wk_ref,
                 ra_ref, rb_ref, rc_ref, rd_ref,
                 qa_ref, ql_ref, qr_ref, qc_ref, iq_ref, iw_ref, qd_ref,
                 ka16_ref, kb16_ref, kc16_ref, kd16_ref):
    def grp(off, g=0):
        return p_ref[:, off + g * LANES: off + (g + 1) * LANES]

    lane = _lane((TM_POST, LANES))
    for g in range(4):
        qa_ref[:, g * LANES:(g + 1) * LANES] = _rope(grp(OFF_QA, g), tab_ref, T_A, ROT_A // 2)
        qc_ref[:, g * LANES:(g + 1) * LANES] = _rope(grp(OFF_QC, g), tab_ref, T_C, ROT_C // 2)
        qd_ref[:, g * LANES:(g + 1) * LANES] = _rope(grp(OFF_QD, g), tab_ref, T_C, ROT_D // 2)
    for g in range(2):
        iq_ref[:, g * LANES:(g + 1) * LANES] = _rope(grp(OFF_IQ, g), tab_ref, T_A, ROT_IDX // 2)

    cq = p_ref[:, OFF_CQ:OFF_CQ + Q_LORA]
    qbh = _dot(_rms(cq, qn_ref[...]).astype(BF16), wuq_ref[...])
    ql_ref[...] = _dot(qbh[:, :H_B * NOPE_B].astype(BF16), wk_ref[...])
    qr_ref[...] = _rope(qbh[:, H_B * NOPE_B:], tab_ref, T_B, ROPE_B // 2)

    ra = _rope(grp(OFF_RA), tab_ref, T_RA, ROT_A // 2)
    rc = _rope(grp(OFF_RC), tab_ref, T_RC, ROT_C // 2)
    rd = _rope(grp(OFF_RD), tab_ref, T_RC, ROT_D // 2)
    ckv = _rms(grp(OFF_CKV), kvn_ref[...])
    m = grp(OFF_MISC)
    is_ik = (lane >= ROPE_B) & (lane < ROPE_B + D_IDX)
    ms = jnp.sum(jnp.where(is_ik, m * m, 0.0), axis=-1, keepdims=True) * (1.0 / D_IDX)
    m = jnp.where(is_ik, m * lax.rsqrt(ms + EPS) * ikn_ref[...], m)
    sl = T_M
    m = (m * tab_ref[:, sl * LANES:(sl + 1) * LANES]
         + pltpu.roll(m, LANES - ROPE_B // 2, 1) * tab_ref[:, (sl + 1) * LANES:(sl + 2) * LANES]
         + pltpu.roll(m, ROPE_B // 2, 1) * tab_ref[:, (sl + 2) * LANES:(sl + 3) * LANES]
         + pltpu.roll(m, LANES - ROT_IDX // 2, 1) * tab_ref[:, (sl + 3) * LANES:(sl + 4) * LANES]
         + pltpu.roll(m, ROT_IDX // 2, 1) * tab_ref[:, (sl + 4) * LANES:(sl + 5) * LANES])
    kr = jnp.where(lane < ROPE_B, m, 0.0)
    ik = jnp.where(lane < D_IDX, pltpu.roll(m, LANES - ROPE_B, 1), 0.0)
    iw = pltpu.roll(m, LANES - (ROPE_B + D_IDX), 1)
    iw_ref[...] = jnp.where(lane < H_IDX, iw * (H_IDX ** -0.5), 0.0)

    ra_ref[...] = ra
    rb_ref[:, :KV_LORA] = ckv
    rb_ref[:, KV_LORA:] = kr[:, :ROPE_B]
    rc_ref[:, :2 * D_C] = rc
    rc_ref[:, 2 * D_C:] = ik[:, :D_IDX]
    rd_ref[...] = rd
    ka16_ref[...] = ra.astype(BF16)
    kb16_ref[:, :LANES] = ckv.astype(BF16)
    kb16_ref[:, LANES:] = kr.astype(BF16)
    kc16_ref[:, :LANES] = rc.astype(BF16)
    kc16_ref[:, LANES:] = ik.astype(BF16)
    kd16_ref[...] = rd.astype(BF16)


def _post(proj, tab, n_prompt_tiles, tab_tiles, qn, kvn, ikn, wuq, wk):
    n = proj.shape[0]

    def tok(w):
        return pl.BlockSpec((TM_POST, w), lambda i: (i, 0))

    def full(a):
        return pl.BlockSpec(a.shape, lambda i: (0,) * a.ndim)

    def tab_map(i):
        return (jnp.where(i < n_prompt_tiles, i % tab_tiles, tab_tiles), 0)

    widths_f32 = [ROW_A, ROW_B, ROW_C, ROW_D, 512, 512, LANES, 512, 256, LANES, 512]
    widths_b16 = [LANES, 2 * LANES, 2 * LANES, LANES]
    out_shape = ([jax.ShapeDtypeStruct((n, w), F32) for w in widths_f32]
                 + [jax.ShapeDtypeStruct((n, w), BF16) for w in widths_b16])
    return pl.pallas_call(
        _post_kernel,
        out_shape=out_shape,
        grid=(n // TM_POST,),
        in_specs=[tok(IN_COLS_P), pl.BlockSpec((TM_POST, N_TAB * LANES), tab_map),
                  full(qn), full(kvn), full(ikn), full(wuq), full(wk)],
        out_specs=[tok(w) for w in widths_f32 + widths_b16],
        compiler_params=_cparams(("arbitrary",)),
        name="split_rope",
    )(proj, tab, qn, kvn, ikn, wuq, wk)


def _softmax_pv(q16, pieces, scale):
    ss = []
    m = None
    for kv16, bias in pieces:
        s = _dot_nt(q16, kv16) * scale
        if bias is not None:
            s = s + bias
        ss.append(s)
        pm = jnp.max(s, axis=-1, keepdims=True)
        m = pm if m is None else jnp.maximum(m, pm)
    l = None
    pv = None
    for s, (kv16, _) in zip(ss, pieces):
        e = jnp.exp(s - m)
        pl_ = jnp.sum(e, axis=-1, keepdims=True)
        pp = _dot(e.astype(BF16), kv16)
        l = pl_ if l is None else l + pl_
        pv = pp if pv is None else pv + pp
    return pv, l


def _take_head(x_ref, col, width, dst, t):
    g, o = divmod(col, LANES)
    x = x_ref[:, g * LANES:(g + 1) * LANES]
    shift = (dst - o) % LANES
    if shift:
        x = pltpu.roll(x, shift, 1)
    lane = _lane((t, LANES))
    return jnp.where((lane >= dst) & (lane < dst + width), x, 0.0)


def _stack_bias(bias, reps):
    return bias if reps == 1 else jnp.concatenate([bias] * reps, axis=0)


def _mixer_a(qa_ref, heads, t, pieces_fn, lam, lam_init, subln):
    q1 = [_take_head(qa_ref, h * 2 * D_A, D_A, 0, t) for h in heads]
    q2 = [_take_head(qa_ref, h * 2 * D_A + D_A, D_A, D_A, t) for h in heads]
    q16 = jnp.concatenate(q1 + q2, axis=0).astype(BF16)
    n = len(heads) * t
    pv, l = _softmax_pv(q16, pieces_fn(2 * len(heads)), D_A ** -0.5)
    o = pv[:n] / l[:n] - lam * (pv[n:] / l[n:])
    lane = _lane((n, LANES))
    o = jnp.where(lane >= 2 * D_A, o, 0.0)
    ms = jnp.sum(o * o, axis=-1, keepdims=True) * (1.0 / (2 * D_A))
    y = o * lax.rsqrt(ms + EPS) * subln * (1.0 - lam_init)
    return [y[i * t:(i + 1) * t] for i in range(len(heads))]


def _mixer_b(ql_ref, qr_ref, heads, t, pieces_fn, wv_ref):
    qs = []
    for h in heads:
        qs.append(jnp.concatenate([ql_ref[:, h * KV_LORA:(h + 1) * KV_LORA],
                                   _take_head(qr_ref, h * ROPE_B, ROPE_B, 0, t)], axis=1))
    q16 = jnp.concatenate(qs, axis=0).astype(BF16)
    pv, l = _softmax_pv(q16, pieces_fn(len(heads)), (NOPE_B + ROPE_B) ** -0.5)
    o_lat = (pv[:, :KV_LORA] / l).astype(BF16)
    return [_dot(o_lat[i * t:(i + 1) * t], wv_ref[h]) for i, h in enumerate(heads)]


def _indexer_scores(iq_ref, iw_ref, t, kv_list):
    zero = jnp.zeros((t, LANES), F32)
    q16 = jnp.concatenate(
        [jnp.concatenate([zero, _take_head(iq_ref, h * D_IDX, D_IDX, 0, t)], axis=1) for h in range(H_IDX)],
        axis=0).astype(BF16)
    scores = []
    for kv16 in kv_list:
        rel = jnp.maximum(_dot_nt(q16, kv16) * (D_IDX ** -0.5), 0.0)
        score = None
        for h in range(H_IDX):
            part = rel[h * t:(h + 1) * t] * iw_ref[:, h:h + 1]
            score = part if score is None else score + part
        scores.append(score)
    return scores


def _topk_mask(score, valid, n_keep):
    t, ln = score.shape
    bits = pltpu.bitcast(score + 0.0, jnp.int32)
    key = bits ^ ((bits >> 31) & 0x7FFFFFFF)
    key = jnp.where(valid, key, INT_MIN)

    def count(m):
        return jnp.sum(jnp.where(m, 1.0, 0.0), axis=-1, keepdims=True)

    thr = jnp.full((t, 1), INT_MIN, jnp.int32)
    for bit in range(31, -1, -1):
        cand = jnp.zeros((t, 1), jnp.int32) if bit == 31 else thr + (1 << bit)
        thr = jnp.where(count(key >= cand) >= n_keep, cand, thr)
    gt = key > thr
    eq = key == thr
    need = n_keep - count(gt)
    col = _lane((t, ln))
    last = jnp.zeros((t, 1), jnp.int32)
    for bit in range(int(ln - 1).bit_length() - 1, -1, -1):
        cand = last + (1 << bit)
        last = jnp.where(count(eq & (col < cand)) < need, cand, last)
    return valid & (gt | (eq & (col <= last)))


def _mixer_c(qc_ref, heads, t, pieces_fn):
    qs = [jnp.concatenate([_take_head(qc_ref, h * D_C, D_C, 0, t), jnp.zeros((t, LANES), F32)], axis=1)
          for h in heads]
    q16 = jnp.concatenate(qs, axis=0).astype(BF16)
    pv, l = _softmax_pv(q16, pieces_fn(len(heads)), D_C ** -0.5)
    o = pv[:, :LANES] / l
    lane = _lane(o.shape)
    o = jnp.where(lane >= D_C, o, 0.0)
    return [o[i * t:(i + 1) * t] for i in range(len(heads))]


def _moba_select(qd_ref, heads, t, kmean16, own):
    qs = [_take_head(qd_ref, h * D_D, D_D, 0, t) for h in heads]
    q = jnp.concatenate(qs, axis=0)
    gate = _dot_nt(q.astype(BF16), kmean16)
    r = gate.shape[0]
    lane = _lane((r, LANES))
    ok = lane < own
    gate = jnp.where(ok, gate, NEG_INF)
    lane_f = lane.astype(F32)
    sel = jnp.zeros((r, LANES), F32)
    for _ in range(MOBA_TOPK):
        m = jnp.max(gate, axis=-1, keepdims=True)
        first = jnp.min(jnp.where(gate == m, lane_f, float(LANES)), axis=-1, keepdims=True)
        hit = lane_f == first
        sel = jnp.where(hit, 1.0, sel)
        gate = jnp.where(hit, NEG_INF, gate)
    return q, jnp.where(ok, sel, 0.0).astype(BF16)


def _mixer_d(q, n_heads, t, pieces):
    pv, l = _softmax_pv(q.astype(BF16), pieces, D_D ** -0.5)
    o = pv / l
    lane = _lane(o.shape)
    o = jnp.where(lane >= D_D, o, 0.0)
    return [o[i * t:(i + 1) * t] for i in range(n_heads)]


def _pair_store(o_ref, ys):
    for g in range(len(ys) // 2):
        o_ref[:, g * LANES:(g + 1) * LANES] = (pltpu.roll(ys[2 * g], D_C, 1) + ys[2 * g + 1]).astype(o_ref.dtype)


def _bias(mask):
    return jnp.where(mask, 0.0, NEG_INF)


def _lam(lv_ref, lam_init):
    lv = lv_ref[...]
    a = jnp.sum(lv[0:1] * lv[1:2], axis=-1, keepdims=True)
    b = jnp.sum(lv[2:3] * lv[3:4], axis=-1, keepdims=True)
    return jnp.exp(a) - jnp.exp(b) + lam_init


def _causal_bias(lk):
    q0 = pl.program_id(1) * TQ
    return _bias(_lane((TQ, lk)) <= q0 + _row((TQ, lk)))


def _prompt_a_kernel(lam_init, qa_ref, k_ref, lv_ref, g_ref, o_ref):
    lk = k_ref.shape[0]
    cb = _causal_bias(lk)
    lam = _lam(lv_ref, lam_init)
    kv = k_ref[...]
    ys = []
    for h in range(H_A):
        ys += _mixer_a(qa_ref, [h], TQ, lambda reps: [(kv, _stack_bias(cb, reps))], lam, lam_init, g_ref[...])
    _pair_store(o_ref, ys)


def _prompt_b_kernel(ql_ref, qr_ref, k_ref, wv_ref, o_ref):
    lk = k_ref.shape[0]
    cb = _causal_bias(lk)
    kv = k_ref[...]
    for h in range(H_B):
        (y,) = _mixer_b(ql_ref, qr_ref, [h], TQ, lambda reps: [(kv, _stack_bias(cb, reps))], wv_ref)
        o_ref[:, h * V_B:(h + 1) * V_B] = y.astype(o_ref.dtype)


def _prompt_c_kernel(qc_ref, iq_ref, iw_ref, k_ref, o_ref):
    lk = k_ref.shape[0]
    q0 = pl.program_id(1) * TQ
    visible = _lane((TQ, lk)) <= q0 + _row((TQ, lk))
    kv = k_ref[...]
    (score,) = _indexer_scores(iq_ref, iw_ref, TQ, [kv])
    sb = _bias(_topk_mask(score, visible, min(DSA_TOPK, lk // 4)))
    ys = []
    for h in range(H_C):
        ys += _mixer_c(qc_ref, [h], TQ, lambda reps: [(kv, _stack_bias(sb, reps))])
    _pair_store(o_ref, ys)


def _block_means(rows, n_blocks):
    w = rows.shape[1]
    km = jnp.sum(rows.reshape(n_blocks, MOBA_BLOCK, w), axis=1) * (1.0 / MOBA_BLOCK)
    return jnp.concatenate([km, jnp.zeros((LANES - n_blocks, w), F32)], axis=0)


def _prompt_d_kernel(qd_ref, k_ref, kf_ref, e_ref, o_ref, km_ref):
    lk = k_ref.shape[0]
    i = pl.program_id(1)

    @pl.when(i == 0)
    def _():
        km_ref[...] = _block_means(kf_ref[...], lk // MOBA_BLOCK).astype(BF16)

    q0 = i * TQ
    col = _lane((TQ, lk))
    qpos = q0 + _row((TQ, lk))
    own_mask = ((col >> MOBA_SHIFT) == (qpos >> MOBA_SHIFT)) & (col <= qpos)
    own = (q0 + _row((TQ, 1))) >> MOBA_SHIFT
    kv = k_ref[...]
    ys = []
    for h in range(H_D):
        q, sel = _moba_select(qd_ref, [h], TQ, km_ref[...], own)
        allowed = (_dot(sel, e_ref[...]) > 0.5) | own_mask
        ys += _mixer_d(q, 1, TQ, [(kv, _bias(allowed))])
    _pair_store(o_ref, ys)


def _prompt_call(kernel, n_batch, seq, q_arrays, kv_arrays, consts, name, scratch=()):
    nq = seq // TQ

    def qspec(a):
        return pl.BlockSpec((TQ, a.shape[1]), lambda b, i: (b * nq + i, 0))

    def kspec(a):
        return pl.BlockSpec((seq, a.shape[1]), lambda b, i: (b, 0))

    def cspec(a):
        return pl.BlockSpec(a.shape, lambda b, i: (0,) * a.ndim)

    return pl.pallas_call(
        kernel,
        out_shape=jax.ShapeDtypeStruct((n_batch * seq, GROUP_W), BF16),
        grid=(n_batch, nq),
        in_specs=[qspec(a) for a in q_arrays] + [kspec(a) for a in kv_arrays] + [cspec(a) for a in consts],
        out_specs=pl.BlockSpec((TQ, GROUP_W), lambda b, i: (b * nq + i, 0)),
        scratch_shapes=list(scratch),
        compiler_params=_cparams(("arbitrary", "arbitrary")),
        name=name,
    )(*q_arrays, *kv_arrays, *consts)


def _paged_fetch(layer, pt_ref, cache_ref, buf_ref, sem_ref, n_pages, page):
    b = pl.program_id(0)

    def copies(seq, slot):
        return [pltpu.make_async_copy(cache_ref.at[layer, pt_ref[seq, p]],
                                      buf_ref.at[slot, pl.ds(p * page, page)],
                                      sem_ref.at[slot]) for p in range(n_pages)]

    @pl.when(b == 0)
    def _():
        for c in copies(0, 0):
            c.start()

    @pl.when(b + 1 < pl.num_programs(0))
    def _():
        for c in copies(b + 1, (b + 1) % 2):
            c.start()

    slot = b % 2
    for c in copies(b, slot):
        c.wait()
    return slot


def _fill_keys(kv_ref, buf_ref, slot, new_ref, past, width):
    chunk = 512

    def body(c, carry):
        r = pl.multiple_of(c * chunk, chunk)
        kv_ref[pl.ds(r, chunk), :width] = buf_ref[slot, pl.ds(r, chunk), :].astype(BF16)
        return carry

    lax.fori_loop(0, past // chunk, body, 0)
    t = new_ref.shape[0]
    kv_ref[past:past + t, :width] = new_ref[...].astype(BF16)


def _new_bias(t, reps):
    r = reps * t
    return _bias(_lane((r, LANES)) <= _row((r, LANES)) % t)


def _zero_tail(kv_ref, past, t):
    @pl.when(pl.program_id(0) == 0)
    def _():
        kv_ref[...] = jnp.zeros(kv_ref.shape, kv_ref.dtype)


def _sample_a_kernel(layer, lam_init, n_pages, page, pt_ref, qa_ref, new_ref, lv_ref, g_ref, cache_ref,
                     o_ref, buf_ref, sem_ref, kv_ref):
    past = n_pages * page
    t = qa_ref.shape[0]
    _zero_tail(kv_ref, past, t)
    slot = _paged_fetch(layer, pt_ref, cache_ref, buf_ref, sem_ref, n_pages, page)
    _fill_keys(kv_ref, buf_ref, slot, new_ref, past, ROW_A)
    lam = _lam(lv_ref, lam_init)

    def pieces(reps):
        return [(kv_ref[:past], None), (kv_ref[past:], _new_bias(t, reps))]

    ys = _mixer_a(qa_ref, list(range(H_A)), t, pieces, lam, lam_init, g_ref[...])
    _pair_store(o_ref, ys)


def _sample_b_kernel(layer, n_pages, page, pt_ref, ql_ref, qr_ref, new_ref, wv_ref, cache_ref,
                     o_ref, buf_ref, sem_ref, kv_ref):
    past = n_pages * page
    t = ql_ref.shape[0]
    _zero_tail(kv_ref, past, t)
    slot = _paged_fetch(layer, pt_ref, cache_ref, buf_ref, sem_ref, n_pages, page)
    _fill_keys(kv_ref, buf_ref, slot, new_ref, past, ROW_B)

    def pieces(reps):
        return [(kv_ref[:past], None), (kv_ref[past:], _new_bias(t, reps))]

    ys = _mixer_b(ql_ref, qr_ref, list(range(H_B)), t, pieces, wv_ref)
    for h, y in enumerate(ys):
        o_ref[:, h * V_B:(h + 1) * V_B] = y.astype(o_ref.dtype)


def _sample_c_kernel(layer, n_pages, page, pt_ref, qc_ref, iq_ref, iw_ref, new_ref, cache_ref,
                     o_ref, buf_ref, sem_ref, kv_ref):
    past = n_pages * page
    t = qc_ref.shape[0]
    _zero_tail(kv_ref, past, t)
    slot = _paged_fetch(layer, pt_ref, cache_ref, buf_ref, sem_ref, n_pages, page)
    _fill_keys(kv_ref, buf_ref, slot, new_ref, past, ROW_C)
    s_past, s_new = _indexer_scores(iq_ref, iw_ref, t, [kv_ref[:past], kv_ref[past:]])
    score = jnp.concatenate([s_past, s_new], axis=1)
    col = _lane(score.shape)
    visible = col <= past + _row(score.shape)
    sb = _bias(_topk_mask(score, visible, min(DSA_TOPK, (past + t) // 4)))

    def pieces(reps):
        return [(kv_ref[:past], _stack_bias(sb[:, :past], reps)), (kv_ref[past:], _stack_bias(sb[:, past:], reps))]

    ys = _mixer_c(qc_ref, list(range(H_C)), t, pieces)
    _pair_store(o_ref, ys)


def _sample_d_kernel(layer, n_pages, page, pt_ref, qd_ref, new_ref, e_ref, cache_ref,
                     o_ref, buf_ref, sem_ref, kv_ref):
    past = n_pages * page
    t = qd_ref.shape[0]
    _zero_tail(kv_ref, past, t)
    slot = _paged_fetch(layer, pt_ref, cache_ref, buf_ref, sem_ref, n_pages, page)
    _fill_keys(kv_ref, buf_ref, slot, new_ref, past, ROW_D)
    km16 = _block_means(buf_ref[slot], past // MOBA_BLOCK).astype(BF16)
    own = jnp.full((H_D * t, 1), past // MOBA_BLOCK, jnp.int32)
    q, sel = _moba_select(qd_ref, list(range(H_D)), t, km16, own)
    allowed = _dot(sel, e_ref[...]) > 0.5
    ys = _mixer_d(q, H_D, t, [(kv_ref[:past], _bias(allowed)), (kv_ref[past:], _new_bias(t, H_D))])
    _pair_store(o_ref, ys)


def _sample_call(kernel, page_table, cache, tok_off, t, q_arrays, new_rows, consts, kv_width, name):
    n_seq, n_pages = page_table.shape
    page, width = cache.shape[2], cache.shape[3]
    past = n_pages * page
    off = tok_off // t

    def tspec(a):
        return pl.BlockSpec((t, a.shape[1]), lambda b, pt: (b + off, 0))

    def cspec(a):
        return pl.BlockSpec(a.shape, lambda b, pt: (0,) * a.ndim)

    grid_spec = pltpu.PrefetchScalarGridSpec(
        num_scalar_prefetch=1,
        grid=(n_seq,),
        in_specs=([tspec(a) for a in q_arrays] + [tspec(new_rows)] + [cspec(a) for a in consts]
                  + [pl.BlockSpec(memory_space=pl.ANY)]),
        out_specs=pl.BlockSpec((t, GROUP_W), lambda b, pt: (b, 0)),
        scratch_shapes=[pltpu.VMEM((2, past, width), F32),
                        pltpu.SemaphoreType.DMA((2,)),
                        pltpu.VMEM((past + LANES, kv_width), BF16)],
    )
    return pl.pallas_call(
        functools.partial(kernel, n_pages, page),
        out_shape=jax.ShapeDtypeStruct((n_seq * t, GROUP_W), F32),
        grid_spec=grid_spec,
        compiler_params=_cparams(("arbitrary",)),
        name=name,
    )(page_table, *q_arrays, new_rows, *consts, cache)


def _rope_tables(pos):
    lane = np.arange(LANES)

    def pattern(dh, n_rot, lo, hi):
        half = n_rot // 2
        inv = jnp.exp(-math.log(ROPE_THETA) * jnp.arange(half, dtype=F32) / half)
        ang = pos.astype(F32)[:, None] * inv
        cos, sin = jnp.cos(ang), jnp.sin(ang)
        r = lane % dh
        act = (lane >= lo) & (lane < hi)
        idx = r % half
        c = jnp.where((r < n_rot) & act, cos[:, idx], 1.0)
        s1 = jnp.where((r < half) & act, -sin[:, idx], 0.0)
        s2 = jnp.where((r >= half) & (r < n_rot) & act, sin[:, idx], 0.0)
        return c, s1, s2

    a = pattern(D_A, ROT_A, 0, LANES)
    c = pattern(D_C, ROT_C, 0, LANES)
    ra = pattern(D_A, ROT_A, 0, 2 * D_A)
    rc = pattern(D_C, ROT_C, 0, D_C)
    b = pattern(ROPE_B, ROPE_B, 0, LANES)
    mb = pattern(ROPE_B, ROPE_B, 0, ROPE_B)
    mi = pattern(D_IDX, ROT_IDX, ROPE_B, ROPE_B + D_IDX)
    misc = (mb[0] * mi[0], mb[1], mb[2], mi[1], mi[2])
    return jnp.concatenate(list(a + c + ra + rc + b + misc), axis=1)


def _prep_layer_weights(w_in, w_out, mla_w_uq, mla_w_ukv, w_up, w_down):
    cuts = np.cumsum([0, H_A * 2 * D_A, 2 * D_A, 2 * D_A, Q_LORA, KV_LORA, ROPE_B,
                      H_C * D_C, D_C, D_C, H_IDX * D_IDX, D_IDX, H_IDX, H_D * D_D, D_D, D_D])
    names = ["qa", "ka", "va", "cq", "ckv", "kr", "qc", "kc", "vc", "iq", "ik", "iw", "qd", "kd", "vd"]
    seg = {n: w_in[..., int(cuts[i]):int(cuts[i + 1])] for i, n in enumerate(names)}
    pad = jnp.zeros(w_in.shape[:-1] + (LANES - ROPE_B - D_IDX - H_IDX,), w_in.dtype)
    order = ["qa", "qc", "qd", "iq", "cq", "ka", "va", "ckv", "kr", "ik", "iw", None, "kc", "vc", "kd", "vd"]
    w_in_p = jnp.concatenate([pad if n is None else seg[n] for n in order], axis=-1).astype(BF16)
    depth = w_in.shape[0]
    uq = mla_w_uq.reshape(depth, Q_LORA, H_B, NOPE_B + ROPE_B)
    w_uq_p = jnp.concatenate([uq[..., :NOPE_B].reshape(depth, Q_LORA, H_B * NOPE_B),
                              uq[..., NOPE_B:].reshape(depth, Q_LORA, H_B * ROPE_B)], axis=-1).astype(BF16)
    wk = jnp.transpose(mla_w_ukv[..., :NOPE_B], (0, 2, 3, 1))
    eye = jnp.eye(H_B, dtype=wk.dtype)
    wk_bd = jnp.einsum("lhnc,hg->lhngc", wk, eye).reshape(depth, H_B * NOPE_B, H_B * KV_LORA).astype(BF16)
    wv = jnp.transpose(mla_w_ukv[..., NOPE_B:], (0, 2, 1, 3)).astype(BF16)
    return w_in_p, w_out.astype(BF16), w_uq_p, wk_bd, wv, w_up.astype(BF16), w_down.astype(BF16)


def kernel(x_prompt, x_sample, cache_diff, cache_mla, cache_dsa, cache_moba, page_table, norm_mix, norm_mlp,
           norm_final, w_in, w_out, diff_lambda, diff_subln, mla_q_norm, mla_kv_norm, mla_w_uq, mla_w_ukv,
           dsa_idx_k_norm, w_up, w_down):
    n_b, seq, d = x_prompt.shape
    n_seq, t_dec, _ = x_sample.shape
    depth = w_in.shape[0]
    n_pages = page_table.shape[1]
    page = cache_diff.shape[2]
    past = n_pages * page
    n_p = n_b * seq
    n_s = n_seq * t_dec
    assert n_p % TM_PROJ == 0 and n_s % TM_PROJ == 0 and seq % TM_POST == 0 and n_s % TM_POST == 0
    assert seq % MOBA_BLOCK == 0 and past % MOBA_BLOCK == 0 and TM_POST % t_dec == 0

    w_in_p, w_out16, w_uq_p, wk_bd, wv, w_up16, w_down16 = _prep_layer_weights(
        w_in, w_out, mla_w_uq, mla_w_ukv, w_up, w_down)
    pos_tab = jnp.concatenate([jnp.arange(seq), past + jnp.arange(TM_POST) % t_dec])
    tab = _rope_tables(pos_tab)
    lane = np.arange(LANES)
    e_prompt = jnp.asarray((np.arange(seq)[None, :] // MOBA_BLOCK) == lane[:, None], BF16)
    e_sample = jnp.asarray((np.arange(past)[None, :] // MOBA_BLOCK) == lane[:, None], BF16)

    h = jnp.concatenate([x_prompt.reshape(n_p, d), x_sample.reshape(n_s, d)], axis=0)
    rows_p = [[], [], [], []]
    rows_s = [[], [], [], []]
    for l in range(depth):
        lam_init = 0.8 - 0.6 * math.exp(-0.3 * l)
        proj = _norm_matmul(h, norm_mix[l][None], w_in_p[l])
        ikn = jnp.zeros((1, LANES), F32).at[0, ROPE_B:ROPE_B + D_IDX].set(dsa_idx_k_norm[l])
        (ra, rb, rc, rd, qa, ql, qr, qc, iq, iw, qd, ka16, kb16, kc16, kd16) = _post(
            proj, tab, n_p // TM_POST, seq // TM_POST, mla_q_norm[l][None], mla_kv_norm[l][None], ikn,
            w_uq_p[l], wk_bd[l])
        subln = jnp.zeros((1, LANES), F32).at[0, 2 * D_A:].set(diff_subln[l])
        lv = diff_lambda[l]

        mp = [
            _prompt_call(functools.partial(_prompt_a_kernel, lam_init), n_b, seq, [qa], [ka16], [lv, subln],
                         "prompt_diff"),
            _prompt_call(_prompt_b_kernel, n_b, seq, [ql, qr], [kb16], [wv[l]], "prompt_mla"),
            _prompt_call(_prompt_c_kernel, n_b, seq, [qc, iq, iw], [kc16], [], "prompt_dsa"),
            _prompt_call(_prompt_d_kernel, n_b, seq, [qd], [kd16, rd], [e_prompt], "prompt_moba",
                         scratch=[pltpu.VMEM((LANES, LANES), BF16)]),
        ]
        ms = [
            _sample_call(functools.partial(_sample_a_kernel, l, lam_init), page_table, cache_diff, n_p, t_dec,
                         [qa], ra, [lv, subln], LANES, "sample_diff"),
            _sample_call(functools.partial(_sample_b_kernel, l), page_table, cache_mla, n_p, t_dec,
                         [ql, qr], rb, [wv[l]], 2 * LANES, "sample_mla"),
            _sample_call(functools.partial(_sample_c_kernel, l), page_table, cache_dsa, n_p, t_dec,
                         [qc, iq, iw], rc, [], 2 * LANES, "sample_dsa"),
            _sample_call(functools.partial(_sample_d_kernel, l), page_table, cache_moba, n_p, t_dec,
                         [qd], rd, [e_sample], LANES, "sample_moba"),
        ]
        mixed = jnp.concatenate([jnp.concatenate(mp, axis=1),
                                 jnp.concatenate(ms, axis=1).astype(BF16)], axis=0)
        h = _matmul_res(mixed, w_out16[l], h)
        h = _mlp(h, norm_mlp[l][None], w_up16[l], w_down16[l])
        for j, r in enumerate((ra, rb, rc, rd)):
            rows_p[j].append(r[:n_p].reshape(n_b, seq, r.shape[1]))
            rows_s[j].append(r[n_p:].reshape(n_seq, t_dec, r.shape[1]))

    y = _final_norm(h, norm_final[None])
    outs = [y[:n_p].reshape(n_b, seq, d), y[n_p:].reshape(n_seq, t_dec, d)]
    for j in range(4):
        outs += [jnp.stack(rows_p[j]), jnp.stack(rows_s[j])]
    return tuple(outs)
```

```python
import functools
import math

import numpy as np
import jax
import jax.numpy as jnp
from jax import lax
from jax.experimental import pallas as pl
from jax.experimental.pallas import tpu as pltpu

F32 = jnp.float32
BF16 = jnp.bfloat16
NEG_INF = float("-inf")
INT_MIN = -(2 ** 31)

D_MODEL = 2048
GROUP_W = D_MODEL // 4
D_FF = 4 * D_MODEL
ROPE_THETA = 500000.0
EPS = 1e-6
D_A = 32
H_A = GROUP_W // (2 * D_A)
ROT_A = D_A // 4
V_B = 128
H_B = GROUP_W // V_B
NOPE_B = 64
ROPE_B = 32
Q_LORA = 256
KV_LORA = 128
D_C = 64
H_C = GROUP_W // D_C
ROT_C = D_C // 4
H_IDX = 8
D_IDX = 32
ROT_IDX = D_IDX // 4
DSA_TOPK = 256
D_D = 64
H_D = GROUP_W // D_D
ROT_D = D_D // 4
MOBA_BLOCK = 256
MOBA_SHIFT = 8
MOBA_TOPK = 3
ROW_A = 4 * D_A
ROW_B = KV_LORA + ROPE_B
ROW_C = 2 * D_C + D_IDX
ROW_D = 2 * D_D

LANES = 128

OFF_QA = 0
OFF_QC = 512
OFF_QD = 1024
OFF_IQ = 1536
OFF_CQ = 1792
OFF_RA = 2048
OFF_CKV = 2176
OFF_MISC = 2304
OFF_RC = 2432
OFF_RD = 2560
IN_COLS_P = 2688

T_A, T_C, T_RA, T_RC, T_B, T_M = 0, 3, 6, 9, 12, 15
N_TAB = 20

VMEM_LIMIT = 56 * 1024 * 1024

TM_PROJ = 512
TN_PROJ = 896
TM_POST = 256
TM_MLP = 512
TF_MLP = 512
TQ = 128
KEY_BUCKETS = 4


def _cparams(sem):
    return pltpu.CompilerParams(dimension_semantics=sem, vmem_limit_bytes=VMEM_LIMIT)


def _rms(x, g):
    return x * lax.rsqrt(jnp.mean(x * x, axis=-1, keepdims=True) + EPS) * g


def _lane(shape):
    return lax.broadcasted_iota(jnp.int32, shape, len(shape) - 1)


def _row(shape):
    return lax.broadcasted_iota(jnp.int32, shape, 0)


def _dot_nt(a, b):
    return lax.dot_general(a, b, (((1,), (1,)), ((), ())), preferred_element_type=F32)


def _dot(a, b):
    return jnp.dot(a, b, preferred_element_type=F32)


def _norm_matmul_kernel(x_ref, g_ref, w_ref, o_ref, xn_ref):
    @pl.when(pl.program_id(1) == 0)
    def _():
        xn_ref[...] = _rms(x_ref[...], g_ref[...]).astype(BF16)

    o_ref[...] = _dot(xn_ref[...], w_ref[...])


def _norm_matmul(x, g, w):
    n, d = x.shape
    cols = w.shape[1]
    return pl.pallas_call(
        _norm_matmul_kernel,
        out_shape=jax.ShapeDtypeStruct((n, cols), F32),
        grid=(n // TM_PROJ, cols // TN_PROJ),
        in_specs=[pl.BlockSpec((TM_PROJ, d), lambda i, j: (i, 0)),
                  pl.BlockSpec((1, d), lambda i, j: (0, 0)),
                  pl.BlockSpec((d, TN_PROJ), lambda i, j: (0, j))],
        out_specs=pl.BlockSpec((TM_PROJ, TN_PROJ), lambda i, j: (i, j)),
        scratch_shapes=[pltpu.VMEM((TM_PROJ, d), BF16)],
        compiler_params=_cparams(("arbitrary", "arbitrary")),
        name="norm_in_proj",
    )(x, g, w)


def _matmul_res_kernel(x_ref, w_ref, r_ref, o_ref):
    o_ref[...] = r_ref[...] + _dot(x_ref[...], w_ref[...])


def _matmul_res(x16, w16, res):
    n, k = x16.shape
    cols = w16.shape[1]
    tn = 1024
    return pl.pallas_call(
        _matmul_res_kernel,
        out_shape=jax.ShapeDtypeStruct((n, cols), F32),
        grid=(n // TM_PROJ, cols // tn),
        in_specs=[pl.BlockSpec((TM_PROJ, k), lambda i, j: (i, 0)),
                  pl.BlockSpec((k, tn), lambda i, j: (0, j)),
                  pl.BlockSpec((TM_PROJ, tn), lambda i, j: (i, j))],
        out_specs=pl.BlockSpec((TM_PROJ, tn), lambda i, j: (i, j)),
        compiler_params=_cparams(("arbitrary", "arbitrary")),
        name="out_proj",
    )(x16, w16, res)


def _mlp_kernel(h_ref, g_ref, wu_ref, wd_ref, o_ref, xn_ref, acc_ref):
    f = pl.program_id(1)

    @pl.when(f == 0)
    def _():
        xn_ref[...] = _rms(h_ref[...], g_ref[...]).astype(BF16)
        acc_ref[...] = h_ref[...]

    u = jnp.maximum(_dot(xn_ref[...], wu_ref[...]), 0.0)
    acc_ref[...] += _dot((u * u).astype(BF16), wd_ref[...])

    @pl.when(f == pl.num_programs(1) - 1)
    def _():
        o_ref[...] = acc_ref[...]


def _mlp(h, g, wu16, wd16):
    n, d = h.shape
    ff = wu16.shape[1]
    return pl.pallas_call(
        _mlp_kernel,
        out_shape=jax.ShapeDtypeStruct((n, d), F32),
        grid=(n // TM_MLP, ff // TF_MLP),
        in_specs=[pl.BlockSpec((TM_MLP, d), lambda i, f: (i, 0)),
                  pl.BlockSpec((1, d), lambda i, f: (0, 0)),
                  pl.BlockSpec((d, TF_MLP), lambda i, f: (0, f)),
                  pl.BlockSpec((TF_MLP, d), lambda i, f: (f, 0))],
        out_specs=pl.BlockSpec((TM_MLP, d), lambda i, f: (i, 0)),
        scratch_shapes=[pltpu.VMEM((TM_MLP, d), BF16), pltpu.VMEM((TM_MLP, d), F32)],
        compiler_params=_cparams(("arbitrary", "arbitrary")),
        name="mlp",
    )(h, g, wu16, wd16)


def _final_norm_kernel(x_ref, g_ref, o_ref):
    o_ref[...] = _rms(x_ref[...], g_ref[...])


def _final_norm(x, g):
    n, d = x.shape
    return pl.pallas_call(
        _final_norm_kernel,
        out_shape=jax.ShapeDtypeStruct((n, d), F32),
        grid=(n // TM_PROJ,),
        in_specs=[pl.BlockSpec((TM_PROJ, d), lambda i: (i, 0)),
                  pl.BlockSpec((1, d), lambda i: (0, 0))],
        out_specs=pl.BlockSpec((TM_PROJ, d), lambda i: (i, 0)),
        compiler_params=_cparams(("arbitrary",)),
        name="final_norm",
    )(x, g)


def _rope(x, tab_ref, slot, half):
    c = tab_ref[:, (slot + 0) * LANES:(slot + 1) * LANES]
    s1 = tab_ref[:, (slot + 1) * LANES:(slot + 2) * LANES]
    s2 = tab_ref[:, (slot + 2) * LANES:(slot + 3) * LANES]
    return x * c + pltpu.roll(x, LANES - half, 1) * s1 + pltpu.roll(x, half, 1) * s2


def _post_kernel(p_ref, tab_ref, qn_ref, kvn_ref, ikn_ref, wuq_ref, wk_ref,
                 ra_ref, rb_ref, rc_ref, rd_ref,
                 qa_ref, ql_ref, qr_ref, qc_ref, iq_ref, iw_ref, qd_ref,
                 ka16_ref, kb16_ref, kc16_ref, kd16_ref):
    def grp(off, g=0):
        return p_ref[:, off + g * LANES: off + (g + 1) * LANES]

    lane = _lane((TM_POST, LANES))
    for g in range(4):
        qa_ref[:, g * LANES:(g + 1) * LANES] = _rope(grp(OFF_QA, g), tab_ref, T_A, ROT_A // 2)
        qc_ref[:, g * LANES:(g + 1) * LANES] = _rope(grp(OFF_QC, g), tab_ref, T_C, ROT_C // 2)
        qd_ref[:, g * LANES:(g + 1) * LANES] = _rope(grp(OFF_QD, g), tab_ref, T_C, ROT_D // 2)
    for g in range(2):
        iq_ref[:, g * LANES:(g + 1) * LANES] = _rope(grp(OFF_IQ, g), tab_ref, T_A, ROT_IDX // 2)

    cq = p_ref[:, OFF_CQ:OFF_CQ + Q_LORA]
    qbh = _dot(_rms(cq, qn_ref[...]).astype(BF16), wuq_ref[...])
    ql_ref[...] = _dot(qbh[:, :H_B * NOPE_B].astype(BF16), wk_ref[...])
    qr_ref[...] = _rope(qbh[:, H_B * NOPE_B:], tab_ref, T_B, ROPE_B // 2)

    ra = _rope(grp(OFF_RA), tab_ref, T_RA, ROT_A // 2)
    rc = _rope(grp(OFF_RC), tab_ref, T_RC, ROT_C // 2)
    rd = _rope(grp(OFF_RD), tab_ref, T_RC, ROT_D // 2)
    ckv = _rms(grp(OFF_CKV), kvn_ref[...])
    m = grp(OFF_MISC)
    is_ik = (lane >= ROPE_B) & (lane < ROPE_B + D_IDX)
    ms = jnp.sum(jnp.where(is_ik, m * m, 0.0), axis=-1, keepdims=True) * (1.0 / D_IDX)
    m = jnp.where(is_ik, m * lax.rsqrt(ms + EPS) * ikn_ref[...], m)
    sl = T_M
    m = (m * tab_ref[:, sl * LANES:(sl + 1) * LANES]
         + pltpu.roll(m, LANES - ROPE_B // 2, 1) * tab_ref[:, (sl + 1) * LANES:(sl + 2) * LANES]
         + pltpu.roll(m, ROPE_B // 2, 1) * tab_ref[:, (sl + 2) * LANES:(sl + 3) * LANES]
         + pltpu.roll(m, LANES - ROT_IDX // 2, 1) * tab_ref[:, (sl + 3) * LANES:(sl + 4) * LANES]
         + pltpu.roll(m, ROT_IDX // 2, 1) * tab_ref[:, (sl + 4) * LANES:(sl + 5) * LANES])
    kr = jnp.where(lane < ROPE_B, m, 0.0)
    ik = jnp.where(lane < D_IDX, pltpu.roll(m, LANES - ROPE_B, 1), 0.0)
    iw = pltpu.roll(m, LANES - (ROPE_B + D_IDX), 1)
    iw_ref[...] = jnp.where(lane < H_IDX, iw * (H_IDX ** -0.5), 0.0)

    ra_ref[...] = ra
    rb_ref[:, :KV_LORA] = ckv
    rb_ref[:, KV_LORA:] = kr[:, :ROPE_B]
    rc_ref[:, :2 * D_C] = rc
    rc_ref[:, 2 * D_C:] = ik[:, :D_IDX]
    rd_ref[...] = rd
    ka16_ref[...] = ra.astype(BF16)
    kb16_ref[:, :LANES] = ckv.astype(BF16)
    kb16_ref[:, LANES:] = kr.astype(BF16)
    kc16_ref[:, :LANES] = rc.astype(BF16)
    kc16_ref[:, LANES:] = ik.astype(BF16)
    kd16_ref[...] = rd.astype(BF16)


def _post(proj, tab, n_prompt_tiles, tab_tiles, qn, kvn, ikn, wuq, wk):
    n = proj.shape[0]

    def tok(w):
        return pl.BlockSpec((TM_POST, w), lambda i: (i, 0))

    def full(a):
        return pl.BlockSpec(a.shape, lambda i: (0,) * a.ndim)

    def tab_map(i):
        return (jnp.where(i < n_prompt_tiles, i % tab_tiles, tab_tiles), 0)

    widths_f32 = [ROW_A, ROW_B, ROW_C, ROW_D, 512, 512, LANES, 512, 256, LANES, 512]
    widths_b16 = [LANES, 2 * LANES, 2 * LANES, LANES]
    out_shape = ([jax.ShapeDtypeStruct((n, w), F32) for w in widths_f32]
                 + [jax.ShapeDtypeStruct((n, w), BF16) for w in widths_b16])
    return pl.pallas_call(
        _post_kernel,
        out_shape=out_shape,
        grid=(n // TM_POST,),
        in_specs=[tok(IN_COLS_P), pl.BlockSpec((TM_POST, N_TAB * LANES), tab_map),
                  full(qn), full(kvn), full(ikn), full(wuq), full(wk)],
        out_specs=[tok(w) for w in widths_f32 + widths_b16],
        compiler_params=_cparams(("arbitrary",)),
        name="split_rope",
    )(proj, tab, qn, kvn, ikn, wuq, wk)


def _softmax_pv(q16, pieces, scale, keys_on_lanes=False):
    ss = []
    m = None
    for kv16, bias in pieces:
        s = (_dot(q16, kv16) if keys_on_lanes else _dot_nt(q16, kv16)) * scale
        if bias is not None:
            s = s + bias
        ss.append(s)
        pm = jnp.max(s, axis=-1, keepdims=True)
        m = pm if m is None else jnp.maximum(m, pm)
    l = None
    pv = None
    for s, (kv16, _) in zip(ss, pieces):
        e = jnp.exp(s - m)
        pl_ = jnp.sum(e, axis=-1, keepdims=True)
        pp = _dot_nt(e.astype(BF16), kv16) if keys_on_lanes else _dot(e.astype(BF16), kv16)
        l = pl_ if l is None else l + pl_
        pv = pp if pv is None else pv + pp
    return pv, l


def _take_head(x_ref, col, width, dst, t):
    g, o = divmod(col, LANES)
    x = x_ref[:, g * LANES:(g + 1) * LANES]
    shift = (dst - o) % LANES
    if shift:
        x = pltpu.roll(x, shift, 1)
    lane = _lane((t, LANES))
    return jnp.where((lane >= dst) & (lane < dst + width), x, 0.0)


def _stack_bias(bias, reps):
    return bias if reps == 1 else jnp.concatenate([bias] * reps, axis=0)


def _mixer_a(qa_ref, heads, t, pieces_fn, lam, lam_init, subln):
    q1 = [_take_head(qa_ref, h * 2 * D_A, D_A, 0, t) for h in heads]
    q2 = [_take_head(qa_ref, h * 2 * D_A + D_A, D_A, D_A, t) for h in heads]
    q16 = jnp.concatenate(q1 + q2, axis=0).astype(BF16)
    n = len(heads) * t
    pv, l = _softmax_pv(q16, pieces_fn(2 * len(heads)), D_A ** -0.5)
    o = pv[:n] / l[:n] - lam * (pv[n:] / l[n:])
    lane = _lane((n, LANES))
    o = jnp.where(lane >= 2 * D_A, o, 0.0)
    ms = jnp.sum(o * o, axis=-1, keepdims=True) * (1.0 / (2 * D_A))
    y = o * lax.rsqrt(ms + EPS) * subln * (1.0 - lam_init)
    return [y[i * t:(i + 1) * t] for i in range(len(heads))]


def _mixer_b(ql_ref, qr_ref, heads, t, pieces_fn, wv_ref, keys_on_lanes=False):
    qs = []
    for h in heads:
        qs.append(jnp.concatenate([ql_ref[:, h * KV_LORA:(h + 1) * KV_LORA],
                                   _take_head(qr_ref, h * ROPE_B, ROPE_B, 0, t)], axis=1))
    q16 = jnp.concatenate(qs, axis=0).astype(BF16)
    pv, l = _softmax_pv(q16, pieces_fn(len(heads)), (NOPE_B + ROPE_B) ** -0.5, keys_on_lanes)
    o_lat = (pv[:, :KV_LORA] / l).astype(BF16)
    return [_dot(o_lat[i * t:(i + 1) * t], wv_ref[h]) for i, h in enumerate(heads)]


def _indexer_scores(iq_ref, iw_ref, t, kv_list, keys_on_lanes=False):
    zero = jnp.zeros((t, LANES), F32)
    q16 = jnp.concatenate(
        [jnp.concatenate([zero, _take_head(iq_ref, h * D_IDX, D_IDX, 0, t)], axis=1) for h in range(H_IDX)],
        axis=0).astype(BF16)
    scores = []
    for kv16 in kv_list:
        raw = _dot(q16, kv16) if keys_on_lanes else _dot_nt(q16, kv16)
        rel = jnp.maximum(raw * (D_IDX ** -0.5), 0.0)
        score = None
        for h in range(H_IDX):
            part = rel[h * t:(h + 1) * t] * iw_ref[:, h:h + 1]
            score = part if score is None else score + part
        scores.append(score)
    return scores


def _topk_mask(score, valid, n_keep):
    t, ln = score.shape
    bits = pltpu.bitcast(score + 0.0, jnp.int32)
    key = bits ^ ((bits >> 31) & 0x7FFFFFFF)
    key = jnp.where(valid, key, INT_MIN)

    def count(m):
        return jnp.sum(jnp.where(m, 1.0, 0.0), axis=-1, keepdims=True)

    step_bits = 2 if t <= 16 else 1
    thr = jnp.full((t, 1), INT_MIN, jnp.int32)
    for bit in range(32 - step_bits, -1, -step_bits):
        cands = [thr + (((j << bit) + 2 ** 31) % 2 ** 32 - 2 ** 31) for j in range(1, 1 << step_bits)]
        oks = [count(key >= c) >= n_keep for c in cands]
        for c, ok in zip(cands, oks):
            thr = jnp.where(ok, c, thr)
    gt = key > thr
    eq = key == thr
    n_gt = count(gt)
    need = n_keep - n_gt
    col = _lane((t, ln))

    def split_ties():
        last = jnp.zeros((t, 1), jnp.int32)
        for bit in range(int(ln - 1).bit_length() - 1, -1, -1):
            cand = last + (1 << bit)
            last = jnp.where(count(eq & (col < cand)) < need, cand, last)
        return last

    surplus = jnp.where((count(eq) > need) & (thr > INT_MIN), 1.0, 0.0)
    last = lax.cond(jnp.max(surplus) > 0.0, split_ties, lambda: jnp.full((t, 1), ln, jnp.int32))
    return valid & (gt | (eq & (col <= last)))


def _mixer_c(qc_ref, heads, t, pieces_fn, keys_on_lanes=False):
    qs = [jnp.concatenate([_take_head(qc_ref, h * D_C, D_C, 0, t), jnp.zeros((t, LANES), F32)], axis=1)
          for h in heads]
    q16 = jnp.concatenate(qs, axis=0).astype(BF16)
    pv, l = _softmax_pv(q16, pieces_fn(len(heads)), D_C ** -0.5, keys_on_lanes)
    o = pv[:, :LANES] / l
    lane = _lane(o.shape)
    o = jnp.where(lane >= D_C, o, 0.0)
    return [o[i * t:(i + 1) * t] for i in range(len(heads))]


def _moba_select(qd_ref, heads, t, kmean16, own):
    qs = [_take_head(qd_ref, h * D_D, D_D, 0, t) for h in heads]
    q = jnp.concatenate(qs, axis=0)
    gate = _dot_nt(q.astype(BF16), kmean16)
    r = gate.shape[0]
    lane = _lane((r, LANES))
    ok = lane < own
    gate = jnp.where(ok, gate, NEG_INF)
    lane_f = lane.astype(F32)
    sel = jnp.zeros((r, LANES), F32)
    for _ in range(MOBA_TOPK):
        m = jnp.max(gate, axis=-1, keepdims=True)
        first = jnp.min(jnp.where(gate == m, lane_f, float(LANES)), axis=-1, keepdims=True)
        hit = lane_f == first
        sel = jnp.where(hit, 1.0, sel)
        gate = jnp.where(hit, NEG_INF, gate)
    return q, jnp.where(ok, sel, 0.0).astype(BF16)


def _mixer_d(q, n_heads, t, pieces):
    pv, l = _softmax_pv(q.astype(BF16), pieces, D_D ** -0.5)
    o = pv / l
    lane = _lane(o.shape)
    o = jnp.where(lane >= D_D, o, 0.0)
    return [o[i * t:(i + 1) * t] for i in range(n_heads)]


def _pair_store(o_ref, ys):
    for g in range(len(ys) // 2):
        o_ref[:, g * LANES:(g + 1) * LANES] = (pltpu.roll(ys[2 * g], D_C, 1) + ys[2 * g + 1]).astype(o_ref.dtype)


def _bias(mask):
    return jnp.where(mask, 0.0, NEG_INF)


def _lam(lv_ref, lam_init):
    lv = lv_ref[...]
    a = jnp.sum(lv[0:1] * lv[1:2], axis=-1, keepdims=True)
    b = jnp.sum(lv[2:3] * lv[3:4], axis=-1, keepdims=True)
    return jnp.exp(a) - jnp.exp(b) + lam_init


def _causal_bias(lk):
    q0 = pl.program_id(1) * TQ
    return _bias(_lane((TQ, lk)) <= q0 + _row((TQ, lk)))


def _for_key_prefix(seq, body):
    i = pl.program_id(1)
    per = seq // TQ // KEY_BUCKETS
    for j in range(KEY_BUCKETS):
        @pl.when((i >= j * per) & (i < (j + 1) * per))
        def _(j=j):
            body((j + 1) * per * TQ)


def _prompt_a_kernel(lam_init, qa_ref, k_ref, lv_ref, g_ref, o_ref):
    def body(lk):
        cb = _causal_bias(lk)
        lam = _lam(lv_ref, lam_init)
        kv = k_ref[:lk]
        ys = []
        for h in range(H_A):
            ys += _mixer_a(qa_ref, [h], TQ, lambda reps: [(kv, _stack_bias(cb, reps))], lam, lam_init, g_ref[...])
        _pair_store(o_ref, ys)

    _for_key_prefix(k_ref.shape[0], body)


def _prompt_b_kernel(ql_ref, qr_ref, k_ref, wv_ref, o_ref):
    def body(lk):
        cb = _causal_bias(lk)
        kv = k_ref[:lk]
        for h in range(H_B):
            (y,) = _mixer_b(ql_ref, qr_ref, [h], TQ, lambda reps: [(kv, _stack_bias(cb, reps))], wv_ref)
            o_ref[:, h * V_B:(h + 1) * V_B] = y.astype(o_ref.dtype)

    _for_key_prefix(k_ref.shape[0], body)


def _prompt_c_kernel(qc_ref, iq_ref, iw_ref, k_ref, o_ref):
    seq = k_ref.shape[0]
    n_keep = min(DSA_TOPK, seq // 4)

    def body(lk):
        assert lk >= n_keep
        q0 = pl.program_id(1) * TQ
        visible = _lane((TQ, lk)) <= q0 + _row((TQ, lk))
        kv = k_ref[:lk]
        (score,) = _indexer_scores(iq_ref, iw_ref, TQ, [kv])
        sb = _bias(_topk_mask(score, visible, n_keep))
        ys = []
        for h in range(H_C):
            ys += _mixer_c(qc_ref, [h], TQ, lambda reps: [(kv, _stack_bias(sb, reps))])
        _pair_store(o_ref, ys)

    _for_key_prefix(seq, body)


def _block_means(rows, n_blocks):
    w = rows.shape[1]
    km = jnp.sum(rows.reshape(n_blocks, MOBA_BLOCK, w), axis=1) * (1.0 / MOBA_BLOCK)
    return jnp.concatenate([km, jnp.zeros((LANES - n_blocks, w), F32)], axis=0)


def _prompt_d_kernel(qd_ref, k_ref, kf_ref, e_ref, o_ref, km_ref):
    lk = k_ref.shape[0]
    i = pl.program_id(1)

    @pl.when(i == 0)
    def _():
        km_ref[...] = _block_means(kf_ref[...], lk // MOBA_BLOCK).astype(BF16)

    def body(lk):
        q0 = i * TQ
        col = _lane((TQ, lk))
        qpos = q0 + _row((TQ, lk))
        own_mask = ((col >> MOBA_SHIFT) == (qpos >> MOBA_SHIFT)) & (col <= qpos)
        own = (q0 + _row((TQ, 1))) >> MOBA_SHIFT
        kv = k_ref[:lk]
        ys = []
        for h in range(H_D):
            q, sel = _moba_select(qd_ref, [h], TQ, km_ref[...], own)
            allowed = (_dot(sel, e_ref[:, :lk]) > 0.5) | own_mask
            ys += _mixer_d(q, 1, TQ, [(kv, _bias(allowed))])
        _pair_store(o_ref, ys)

    _for_key_prefix(lk, body)


def _prompt_call(kernel, n_batch, seq, q_arrays, kv_arrays, consts, name, scratch=()):
    nq = seq // TQ

    def qspec(a):
        return pl.BlockSpec((TQ, a.shape[1]), lambda b, i: (b * nq + i, 0))

    def kspec(a):
        return pl.BlockSpec((seq, a.shape[1]), lambda b, i: (b, 0))

    def cspec(a):
        return pl.BlockSpec(a.shape, lambda b, i: (0,) * a.ndim)

    return pl.pallas_call(
        kernel,
        out_shape=jax.ShapeDtypeStruct((n_batch * seq, GROUP_W), BF16),
        grid=(n_batch, nq),
        in_specs=[qspec(a) for a in q_arrays] + [kspec(a) for a in kv_arrays] + [cspec(a) for a in consts],
        out_specs=pl.BlockSpec((TQ, GROUP_W), lambda b, i: (b * nq + i, 0)),
        scratch_shapes=list(scratch),
        compiler_params=_cparams(("arbitrary", "arbitrary")),
        name=name,
    )(*q_arrays, *kv_arrays, *consts)


def _paged_fetch(layer, pt_ref, cache_ref, buf_ref, sem_ref, n_pages, page, keys_on_lanes=False):
    b = pl.program_id(0)

    def copies(seq, slot):
        def dst(p):
            if keys_on_lanes:
                return buf_ref.at[slot, :, pl.ds(p * page, page)]
            return buf_ref.at[slot, pl.ds(p * page, page)]

        return [pltpu.make_async_copy(cache_ref.at[layer, pt_ref[seq, p]], dst(p), sem_ref.at[slot])
                for p in range(n_pages)]

    @pl.when(b == 0)
    def _():
        for c in copies(0, 0):
            c.start()

    @pl.when(b + 1 < pl.num_programs(0))
    def _():
        for c in copies(b + 1, (b + 1) % 2):
            c.start()

    slot = b % 2
    for c in copies(b, slot):
        c.wait()
    return slot


def _fill_keys(kv_ref, buf_ref, slot, new_ref, past, width):
    chunk = 512

    def body(c, carry):
        r = pl.multiple_of(c * chunk, chunk)
        kv_ref[pl.ds(r, chunk), :width] = buf_ref[slot, pl.ds(r, chunk), :].astype(BF16)
        return carry

    lax.fori_loop(0, past // chunk, body, 0)
    t = new_ref.shape[0]
    kv_ref[past:past + t, :width] = new_ref[...].astype(BF16)


def _fill_keys_t(kv_ref, buf_ref, slot, newt_ref, past, width):
    chunk = min(past, 1024)

    def body(c, carry):
        r = pl.multiple_of(c * chunk, chunk)
        kv_ref[:width, pl.ds(r, chunk)] = buf_ref[slot, :, pl.ds(r, chunk)].astype(BF16)
        return carry

    lax.fori_loop(0, past // chunk, body, 0)
    kv_ref[:width, past:] = newt_ref[0].astype(BF16)


def _new_bias(t, reps):
    r = reps * t
    return _bias(_lane((r, LANES)) <= _row((r, LANES)) % t)


def _zero_tail(kv_ref, past, t):
    @pl.when(pl.program_id(0) == 0)
    def _():
        kv_ref[...] = jnp.zeros(kv_ref.shape, kv_ref.dtype)


def _sample_a_kernel(layer, lam_init, n_pages, page, pt_ref, qa_ref, new_ref, lv_ref, g_ref, cache_ref,
                     o_ref, buf_ref, sem_ref, kv_ref):
    past = n_pages * page
    t = qa_ref.shape[0]
    _zero_tail(kv_ref, past, t)
    slot = _paged_fetch(layer, pt_ref, cache_ref, buf_ref, sem_ref, n_pages, page)
    _fill_keys(kv_ref, buf_ref, slot, new_ref, past, ROW_A)
    lam = _lam(lv_ref, lam_init)

    def pieces(reps):
        return [(kv_ref[:past], None), (kv_ref[past:], _new_bias(t, reps))]

    ys = _mixer_a(qa_ref, list(range(H_A)), t, pieces, lam, lam_init, g_ref[...])
    _pair_store(o_ref, ys)


def _sample_b_kernel(layer, n_pages, page, pt_ref, ql_ref, qr_ref, new_ref, wv_ref, cache_ref,
                     o_ref, buf_ref, sem_ref, kv_ref):
    past = n_pages * page
    t = ql_ref.shape[0]
    _zero_tail(kv_ref, past, t)
    slot = _paged_fetch(layer, pt_ref, cache_ref, buf_ref, sem_ref, n_pages, page, keys_on_lanes=True)
    _fill_keys_t(kv_ref, buf_ref, slot, new_ref, past, ROW_B)

    def pieces(reps):
        return [(kv_ref[:, :past], None), (kv_ref[:, past:], _new_bias(t, reps))]

    ys = _mixer_b(ql_ref, qr_ref, list(range(H_B)), t, pieces, wv_ref, keys_on_lanes=True)
    for h, y in enumerate(ys):
        o_ref[:, h * V_B:(h + 1) * V_B] = y.astype(o_ref.dtype)


def _sample_c_kernel(layer, n_pages, page, pt_ref, qc_ref, iq_ref, iw_ref, new_ref, cache_ref,
                     o_ref, buf_ref, sem_ref, kv_ref):
    past = n_pages * page
    t = qc_ref.shape[0]
    _zero_tail(kv_ref, past, t)
    slot = _paged_fetch(layer, pt_ref, cache_ref, buf_ref, sem_ref, n_pages, page, keys_on_lanes=True)
    _fill_keys_t(kv_ref, buf_ref, slot, new_ref, past, ROW_C)
    s_past, s_new = _indexer_scores(iq_ref, iw_ref, t, [kv_ref[:, :past], kv_ref[:, past:]], keys_on_lanes=True)
    score = jnp.concatenate([s_past, s_new], axis=1)
    col = _lane(score.shape)
    visible = col <= past + _row(score.shape)
    sb = _bias(_topk_mask(score, visible, min(DSA_TOPK, (past + t) // 4)))

    def pieces(reps):
        return [(kv_ref[:, :past], _stack_bias(sb[:, :past], reps)),
                (kv_ref[:, past:], _stack_bias(sb[:, past:], reps))]

    ys = _mixer_c(qc_ref, list(range(H_C)), t, pieces, keys_on_lanes=True)
    _pair_store(o_ref, ys)


def _sample_d_kernel(layer, n_pages, page, pt_ref, qd_ref, new_ref, e_ref, cache_ref,
                     o_ref, buf_ref, sem_ref, kv_ref):
    past = n_pages * page
    t = qd_ref.shape[0]
    _zero_tail(kv_ref, past, t)
    slot = _paged_fetch(layer, pt_ref, cache_ref, buf_ref, sem_ref, n_pages, page)
    _fill_keys(kv_ref, buf_ref, slot, new_ref, past, ROW_D)
    km16 = _block_means(buf_ref[slot], past // MOBA_BLOCK).astype(BF16)
    own = jnp.full((H_D * t, 1), past // MOBA_BLOCK, jnp.int32)
    q, sel = _moba_select(qd_ref, list(range(H_D)), t, km16, own)
    allowed = _dot(sel, e_ref[...]) > 0.5
    ys = _mixer_d(q, H_D, t, [(kv_ref[:past], _bias(allowed)), (kv_ref[past:], _new_bias(t, H_D))])
    _pair_store(o_ref, ys)


def _sample_call(kernel, page_table, cache, tok_off, t, q_arrays, new_rows, consts, kv_width, name,
                 keys_on_lanes=False):
    n_seq, n_pages = page_table.shape
    if keys_on_lanes:
        width, page = cache.shape[2], cache.shape[3]
    else:
        page, width = cache.shape[2], cache.shape[3]
    past = n_pages * page
    off = tok_off // t

    def tspec(a):
        return pl.BlockSpec((t, a.shape[1]), lambda b, pt: (b + off, 0))

    def cspec(a):
        return pl.BlockSpec(a.shape, lambda b, pt: (0,) * a.ndim)

    if keys_on_lanes:
        new_spec = pl.BlockSpec((1, width, LANES), lambda b, pt: (b, 0, 0))
        buf_shape, kv_shape = (2, width, past), (kv_width, past + LANES)
    else:
        new_spec = tspec(new_rows)
        buf_shape, kv_shape = (2, past, width), (past + LANES, kv_width)

    grid_spec = pltpu.PrefetchScalarGridSpec(
        num_scalar_prefetch=1,
        grid=(n_seq,),
        in_specs=([tspec(a) for a in q_arrays] + [new_spec] + [cspec(a) for a in consts]
                  + [pl.BlockSpec(memory_space=pl.ANY)]),
        out_specs=pl.BlockSpec((t, GROUP_W), lambda b, pt: (b, 0)),
        scratch_shapes=[pltpu.VMEM(buf_shape, F32),
                        pltpu.SemaphoreType.DMA((2,)),
                        pltpu.VMEM(kv_shape, BF16)],
    )
    return pl.pallas_call(
        functools.partial(kernel, n_pages, page),
        out_shape=jax.ShapeDtypeStruct((n_seq * t, GROUP_W), F32),
        grid_spec=grid_spec,
        compiler_params=_cparams(("arbitrary",)),
        name=name,
    )(page_table, *q_arrays, new_rows, *consts, cache)


def _rope_tables(pos):
    lane = np.arange(LANES)

    def pattern(dh, n_rot, lo, hi):
        half = n_rot // 2
        inv = jnp.exp(-math.log(ROPE_THETA) * jnp.arange(half, dtype=F32) / half)
        ang = pos.astype(F32)[:, None] * inv
        cos, sin = jnp.cos(ang), jnp.sin(ang)
        r = lane % dh
        act = (lane >= lo) & (lane < hi)
        idx = r % half
        c = jnp.where((r < n_rot) & act, cos[:, idx], 1.0)
        s1 = jnp.where((r < half) & act, -sin[:, idx], 0.0)
        s2 = jnp.where((r >= half) & (r < n_rot) & act, sin[:, idx], 0.0)
        return c, s1, s2

    a = pattern(D_A, ROT_A, 0, LANES)
    c = pattern(D_C, ROT_C, 0, LANES)
    ra = pattern(D_A, ROT_A, 0, 2 * D_A)
    rc = pattern(D_C, ROT_C, 0, D_C)
    b = pattern(ROPE_B, ROPE_B, 0, LANES)
    mb = pattern(ROPE_B, ROPE_B, 0, ROPE_B)
    mi = pattern(D_IDX, ROT_IDX, ROPE_B, ROPE_B + D_IDX)
    misc = (mb[0] * mi[0], mb[1], mb[2], mi[1], mi[2])
    return jnp.concatenate(list(a + c + ra + rc + b + misc), axis=1)


def _prep_layer_weights(w_in, w_out, mla_w_uq, mla_w_ukv, w_up, w_down):
    cuts = np.cumsum([0, H_A * 2 * D_A, 2 * D_A, 2 * D_A, Q_LORA, KV_LORA, ROPE_B,
                      H_C * D_C, D_C, D_C, H_IDX * D_IDX, D_IDX, H_IDX, H_D * D_D, D_D, D_D])
    names = ["qa", "ka", "va", "cq", "ckv", "kr", "qc", "kc", "vc", "iq", "ik", "iw", "qd", "kd", "vd"]
    seg = {n: w_in[..., int(cuts[i]):int(cuts[i + 1])] for i, n in enumerate(names)}
    pad = jnp.zeros(w_in.shape[:-1] + (LANES - ROPE_B - D_IDX - H_IDX,), w_in.dtype)
    order = ["qa", "qc", "qd", "iq", "cq", "ka", "va", "ckv", "kr", "ik", "iw", None, "kc", "vc", "kd", "vd"]
    w_in_p = jnp.concatenate([pad if n is None else seg[n] for n in order], axis=-1).astype(BF16)
    depth = w_in.shape[0]
    uq = mla_w_uq.reshape(depth, Q_LORA, H_B, NOPE_B + ROPE_B)
    w_uq_p = jnp.concatenate([uq[..., :NOPE_B].reshape(depth, Q_LORA, H_B * NOPE_B),
                              uq[..., NOPE_B:].reshape(depth, Q_LORA, H_B * ROPE_B)], axis=-1).astype(BF16)
    wk = jnp.transpose(mla_w_ukv[..., :NOPE_B], (0, 2, 3, 1))
    eye = jnp.eye(H_B, dtype=wk.dtype)
    wk_bd = jnp.einsum("lhnc,hg->lhngc", wk, eye).reshape(depth, H_B * NOPE_B, H_B * KV_LORA).astype(BF16)
    wv = jnp.transpose(mla_w_ukv[..., NOPE_B:], (0, 2, 1, 3)).astype(BF16)
    return w_in_p, w_out.astype(BF16), w_uq_p, wk_bd, wv, w_up.astype(BF16), w_down.astype(BF16)


def kernel(x_prompt, x_sample, cache_diff, cache_mla, cache_dsa, cache_moba, page_table, norm_mix, norm_mlp,
           norm_final, w_in, w_out, diff_lambda, diff_subln, mla_q_norm, mla_kv_norm, mla_w_uq, mla_w_ukv,
           dsa_idx_k_norm, w_up, w_down):
    n_b, seq, d = x_prompt.shape
    n_seq, t_dec, _ = x_sample.shape
    depth = w_in.shape[0]
    n_pages = page_table.shape[1]
    page = cache_diff.shape[2]
    past = n_pages * page
    n_p = n_b * seq
    n_s = n_seq * t_dec
    assert n_p % TM_PROJ == 0 and n_s % TM_PROJ == 0 and seq % TM_POST == 0 and n_s % TM_POST == 0
    assert seq % MOBA_BLOCK == 0 and past % MOBA_BLOCK == 0 and TM_POST % t_dec == 0
    assert seq % (TQ * KEY_BUCKETS) == 0

    w_in_p, w_out16, w_uq_p, wk_bd, wv, w_up16, w_down16 = _prep_layer_weights(
        w_in, w_out, mla_w_uq, mla_w_ukv, w_up, w_down)
    pos_tab = jnp.concatenate([jnp.arange(seq), past + jnp.arange(TM_POST) % t_dec])
    tab = _rope_tables(pos_tab)
    lane = np.arange(LANES)
    e_prompt = jnp.asarray((np.arange(seq)[None, :] // MOBA_BLOCK) == lane[:, None], BF16)
    e_sample = jnp.asarray((np.arange(past)[None, :] // MOBA_BLOCK) == lane[:, None], BF16)

    cache_mla_t = jnp.swapaxes(cache_mla, 2, 3)
    cache_dsa_t = jnp.swapaxes(cache_dsa, 2, 3)

    def new_t(rows):
        r = jnp.swapaxes(rows[n_p:].reshape(n_seq, t_dec, rows.shape[1]), 1, 2)
        return jnp.pad(r, ((0, 0), (0, 0), (0, LANES - t_dec)))

    h = jnp.concatenate([x_prompt.reshape(n_p, d), x_sample.reshape(n_s, d)], axis=0)
    rows_p = [[], [], [], []]
    rows_s = [[], [], [], []]
    for l in range(depth):
        lam_init = 0.8 - 0.6 * math.exp(-0.3 * l)
        proj = _norm_matmul(h, norm_mix[l][None], w_in_p[l])
        ikn = jnp.zeros((1, LANES), F32).at[0, ROPE_B:ROPE_B + D_IDX].set(dsa_idx_k_norm[l])
        (ra, rb, rc, rd, qa, ql, qr, qc, iq, iw, qd, ka16, kb16, kc16, kd16) = _post(
            proj, tab, n_p // TM_POST, seq // TM_POST, mla_q_norm[l][None], mla_kv_norm[l][None], ikn,
            w_uq_p[l], wk_bd[l])
        subln = jnp.zeros((1, LANES), F32).at[0, 2 * D_A:].set(diff_subln[l])
        lv = diff_lambda[l]

        mp = [
            _prompt_call(functools.partial(_prompt_a_kernel, lam_init), n_b, seq, [qa], [ka16], [lv, subln],
                         "prompt_diff"),
            _prompt_call(_prompt_b_kernel, n_b, seq, [ql, qr], [kb16], [wv[l]], "prompt_mla"),
            _prompt_call(_prompt_c_kernel, n_b, seq, [qc, iq, iw], [kc16], [], "prompt_dsa"),
            _prompt_call(_prompt_d_kernel, n_b, seq, [qd], [kd16, rd], [e_prompt], "prompt_moba",
                         scratch=[pltpu.VMEM((LANES, LANES), BF16)]),
        ]
        ms = [
            _sample_call(functools.partial(_sample_a_kernel, l, lam_init), page_table, cache_diff, n_p, t_dec,
                         [qa], ra, [lv, subln], LANES, "sample_diff"),
            _sample_call(functools.partial(_sample_b_kernel, l), page_table, cache_mla_t, n_p, t_dec,
                         [ql, qr], new_t(rb), [wv[l]], 2 * LANES, "sample_mla", keys_on_lanes=True),
            _sample_call(functools.partial(_sample_c_kernel, l), page_table, cache_dsa_t, n_p, t_dec,
                         [qc, iq, iw], new_t(rc), [], 2 * LANES, "sample_dsa", keys_on_lanes=True),
            _sample_call(functools.partial(_sample_d_kernel, l), page_table, cache_moba, n_p, t_dec,
                         [qd], rd, [e_sample], LANES, "sample_moba"),
        ]
        mixed = jnp.concatenate([jnp.concatenate(mp, axis=1),
                                 jnp.concatenate(ms, axis=1).astype(BF16)], axis=0)
        h = _matmul_res(mixed, w_out16[l], h)
        h = _mlp(h, norm_mlp[l][None], w_up16[l], w_down16[l])
        for j, r in enumerate((ra, rb, rc, rd)):
            rows_p[j].append(r[:n_p].reshape(n_b, seq, r.shape[1]))
            rows_s[j].append(r[n_p:].reshape(n_seq, t_dec, r.shape[1]))

    y = _final_norm(h, norm_final[None])
    outs = [y[:n_p].reshape(n_b, seq, d), y[n_p:].reshape(n_seq, t_dec, d)]
    for j in range(4):
        outs += [jnp.stack(rows_p[j]), jnp.stack(rows_s[j])]
    return tuple(outs)
```

```python
import functools
import math

import numpy as np
import jax
import jax.numpy as jnp
from jax import lax
from jax.experimental import pallas as pl
from jax.experimental.pallas import tpu as pltpu

F32 = jnp.float32
BF16 = jnp.bfloat16
NEG_INF = float("-inf")
INT_MIN = -(2 ** 31)

D_MODEL = 2048
GROUP_W = D_MODEL // 4
D_FF = 4 * D_MODEL
ROPE_THETA = 500000.0
EPS = 1e-6
D_A = 32
H_A = GROUP_W // (2 * D_A)
ROT_A = D_A // 4
V_B = 128
H_B = GROUP_W // V_B
NOPE_B = 64
ROPE_B = 32
Q_LORA = 256
KV_LORA = 128
D_C = 64
H_C = GROUP_W // D_C
ROT_C = D_C // 4
H_IDX = 8
D_IDX = 32
ROT_IDX = D_IDX // 4
DSA_TOPK = 256
D_D = 64
H_D = GROUP_W // D_D
ROT_D = D_D // 4
MOBA_BLOCK = 256
MOBA_SHIFT = 8
MOBA_TOPK = 3
ROW_A = 4 * D_A
ROW_B = KV_LORA + ROPE_B
ROW_C = 2 * D_C + D_IDX
ROW_D = 2 * D_D

LANES = 128

OFF_QA = 0
OFF_QC = 512
OFF_QD = 1024
OFF_IQ = 1536
OFF_CQ = 1792
OFF_RA = 2048
OFF_CKV = 2176
OFF_MISC = 2304
OFF_RC = 2432
OFF_RD = 2560
IN_COLS_P = 2688

T_A, T_C, T_RA, T_RC, T_B, T_M = 0, 3, 6, 9, 12, 15
N_TAB = 20

VMEM_LIMIT = 56 * 1024 * 1024

TM_PROJ = 512
TN_PROJ = 896
TM_POST = 256
TM_MLP = 512
TF_MLP = 512
TQ = 128
KEY_BUCKETS = 4
HEADS_PER_PASS_A = 2
HEADS_PER_PASS_C = 4
HEADS_PER_PASS_D = 4
assert TQ & (TQ - 1) == 0


def _cparams(sem):
    return pltpu.CompilerParams(dimension_semantics=sem, vmem_limit_bytes=VMEM_LIMIT)


def _rms(x, g):
    return x * lax.rsqrt(jnp.mean(x * x, axis=-1, keepdims=True) + EPS) * g


def _lane(shape):
    return lax.broadcasted_iota(jnp.int32, shape, len(shape) - 1)


def _row(shape):
    return lax.broadcasted_iota(jnp.int32, shape, 0)


def _dot_nt(a, b):
    return lax.dot_general(a, b, (((1,), (1,)), ((), ())), preferred_element_type=F32)


def _dot(a, b):
    return jnp.dot(a, b, preferred_element_type=F32)


def _norm_matmul_kernel(x_ref, g_ref, w_ref, o_ref, xn_ref):
    @pl.when(pl.program_id(1) == 0)
    def _():
        xn_ref[...] = _rms(x_ref[...], g_ref[...]).astype(BF16)

    o_ref[...] = _dot(xn_ref[...], w_ref[...])


def _row_tile(n):
    return 2 * TM_PROJ if n % (2 * TM_PROJ) == 0 else TM_PROJ


def _norm_matmul(x, g, w):
    n, d = x.shape
    cols = w.shape[1]
    tm = _row_tile(n)
    return pl.pallas_call(
        _norm_matmul_kernel,
        out_shape=jax.ShapeDtypeStruct((n, cols), F32),
        grid=(n // tm, cols // TN_PROJ),
        in_specs=[pl.BlockSpec((tm, d), lambda i, j: (i, 0)),
                  pl.BlockSpec((1, d), lambda i, j: (0, 0)),
                  pl.BlockSpec((d, TN_PROJ), lambda i, j: (0, j))],
        out_specs=pl.BlockSpec((tm, TN_PROJ), lambda i, j: (i, j)),
        scratch_shapes=[pltpu.VMEM((tm, d), BF16)],
        compiler_params=_cparams(("arbitrary", "arbitrary")),
        name="norm_in_proj",
    )(x, g, w)


def _matmul_res_kernel(x_ref, w_ref, r_ref, o_ref):
    o_ref[...] = r_ref[...] + _dot(x_ref[...], w_ref[...])


def _matmul_res(x16, w16, res):
    n, k = x16.shape
    cols = w16.shape[1]
    tn = 1024
    tm = _row_tile(n)
    return pl.pallas_call(
        _matmul_res_kernel,
        out_shape=jax.ShapeDtypeStruct((n, cols), F32),
        grid=(n // tm, cols // tn),
        in_specs=[pl.BlockSpec((tm, k), lambda i, j: (i, 0)),
                  pl.BlockSpec((k, tn), lambda i, j: (0, j)),
                  pl.BlockSpec((tm, tn), lambda i, j: (i, j))],
        out_specs=pl.BlockSpec((tm, tn), lambda i, j: (i, j)),
        compiler_params=_cparams(("arbitrary", "arbitrary")),
        name="out_proj",
    )(x16, w16, res)


def _mlp_kernel(h_ref, g_ref, wu_ref, wd_ref, o_ref, xn_ref, acc_ref):
    f = pl.program_id(1)

    @pl.when(f == 0)
    def _():
        xn_ref[...] = _rms(h_ref[...], g_ref[...]).astype(BF16)
        acc_ref[...] = h_ref[...]

    u = jnp.maximum(_dot(xn_ref[...], wu_ref[...]), 0.0)
    acc_ref[...] += _dot((u * u).astype(BF16), wd_ref[...])

    @pl.when(f == pl.num_programs(1) - 1)
    def _():
        o_ref[...] = acc_ref[...]


def _mlp(h, g, wu16, wd16):
    n, d = h.shape
    ff = wu16.shape[1]
    return pl.pallas_call(
        _mlp_kernel,
        out_shape=jax.ShapeDtypeStruct((n, d), F32),
        grid=(n // TM_MLP, ff // TF_MLP),
        in_specs=[pl.BlockSpec((TM_MLP, d), lambda i, f: (i, 0)),
                  pl.BlockSpec((1, d), lambda i, f: (0, 0)),
                  pl.BlockSpec((d, TF_MLP), lambda i, f: (0, f)),
                  pl.BlockSpec((TF_MLP, d), lambda i, f: (f, 0))],
        out_specs=pl.BlockSpec((TM_MLP, d), lambda i, f: (i, 0)),
        scratch_shapes=[pltpu.VMEM((TM_MLP, d), BF16), pltpu.VMEM((TM_MLP, d), F32)],
        compiler_params=_cparams(("arbitrary", "arbitrary")),
        name="mlp",
    )(h, g, wu16, wd16)


def _final_norm_kernel(x_ref, g_ref, o_ref):
    o_ref[...] = _rms(x_ref[...], g_ref[...])


def _final_norm(x, g):
    n, d = x.shape
    return pl.pallas_call(
        _final_norm_kernel,
        out_shape=jax.ShapeDtypeStruct((n, d), F32),
        grid=(n // TM_PROJ,),
        in_specs=[pl.BlockSpec((TM_PROJ, d), lambda i: (i, 0)),
                  pl.BlockSpec((1, d), lambda i: (0, 0))],
        out_specs=pl.BlockSpec((TM_PROJ, d), lambda i: (i, 0)),
        compiler_params=_cparams(("arbitrary",)),
        name="final_norm",
    )(x, g)


def _rope(x, tab_ref, slot, half):
    c = tab_ref[:, (slot + 0) * LANES:(slot + 1) * LANES]
    s1 = tab_ref[:, (slot + 1) * LANES:(slot + 2) * LANES]
    s2 = tab_ref[:, (slot + 2) * LANES:(slot + 3) * LANES]
    return x * c + pltpu.roll(x, LANES - half, 1) * s1 + pltpu.roll(x, half, 1) * s2


def _post_kernel(p_ref, tab_ref, qn_ref, kvn_ref, ikn_ref, wuq_ref, wk_ref,
                 ra_ref, rb_ref, rc_ref, rd_ref,
                 qa_ref, ql_ref, qr_ref, qc_ref, iq_ref, iw_ref, qd_ref,
                 ka16_ref, kb16_ref, kc16_ref, kd16_ref):
    def grp(off, g=0):
        return p_ref[:, off + g * LANES: off + (g + 1) * LANES]

    lane = _lane((TM_POST, LANES))
    for g in range(4):
        qa_ref[:, g * LANES:(g + 1) * LANES] = _rope(grp(OFF_QA, g), tab_ref, T_A, ROT_A // 2)
        qc_ref[:, g * LANES:(g + 1) * LANES] = _rope(grp(OFF_QC, g), tab_ref, T_C, ROT_C // 2)
        qd_ref[:, g * LANES:(g + 1) * LANES] = _rope(grp(OFF_QD, g), tab_ref, T_C, ROT_D // 2)
    for g in range(2):
        iq_ref[:, g * LANES:(g + 1) * LANES] = _rope(grp(OFF_IQ, g), tab_ref, T_A, ROT_IDX // 2)

    cq = p_ref[:, OFF_CQ:OFF_CQ + Q_LORA]
    qbh = _dot(_rms(cq, qn_ref[...]).astype(BF16), wuq_ref[...])
    ql_ref[...] = _dot(qbh[:, :H_B * NOPE_B].astype(BF16), wk_ref[...])
    qr_ref[...] = _rope(qbh[:, H_B * NOPE_B:], tab_ref, T_B, ROPE_B // 2)

    ra = _rope(grp(OFF_RA), tab_ref, T_RA, ROT_A // 2)
    rc = _rope(grp(OFF_RC), tab_ref, T_RC, ROT_C // 2)
    rd = _rope(grp(OFF_RD), tab_ref, T_RC, ROT_D // 2)
    ckv = _rms(grp(OFF_CKV), kvn_ref[...])
    m = grp(OFF_MISC)
    is_ik = (lane >= ROPE_B) & (lane < ROPE_B + D_IDX)
    ms = jnp.sum(jnp.where(is_ik, m * m, 0.0), axis=-1, keepdims=True) * (1.0 / D_IDX)
    m = jnp.where(is_ik, m * lax.rsqrt(ms + EPS) * ikn_ref[...], m)
    sl = T_M
    m = (m * tab_ref[:, sl * LANES:(sl + 1) * LANES]
         + pltpu.roll(m, LANES - ROPE_B // 2, 1) * tab_ref[:, (sl + 1) * LANES:(sl + 2) * LANES]
         + pltpu.roll(m, ROPE_B // 2, 1) * tab_ref[:, (sl + 2) * LANES:(sl + 3) * LANES]
         + pltpu.roll(m, LANES - ROT_IDX // 2, 1) * tab_ref[:, (sl + 3) * LANES:(sl + 4) * LANES]
         + pltpu.roll(m, ROT_IDX // 2, 1) * tab_ref[:, (sl + 4) * LANES:(sl + 5) * LANES])
    kr = jnp.where(lane < ROPE_B, m, 0.0)
    ik = jnp.where(lane < D_IDX, pltpu.roll(m, LANES - ROPE_B, 1), 0.0)
    iw = pltpu.roll(m, LANES - (ROPE_B + D_IDX), 1)
    iw_ref[...] = jnp.where(lane < H_IDX, iw * (H_IDX ** -0.5), 0.0)

    ra_ref[...] = ra
    rb_ref[:, :KV_LORA] = ckv
    rb_ref[:, KV_LORA:] = kr[:, :ROPE_B]
    rc_ref[:, :2 * D_C] = rc
    rc_ref[:, 2 * D_C:] = ik[:, :D_IDX]
    rd_ref[...] = rd
    ka16_ref[...] = ra.astype(BF16)
    kb16_ref[:, :LANES] = ckv.astype(BF16)
    kb16_ref[:, LANES:] = kr.astype(BF16)
    kc16_ref[:, :LANES] = rc.astype(BF16)
    kc16_ref[:, LANES:] = ik.astype(BF16)
    kd16_ref[...] = rd.astype(BF16)


def _post(proj, tab, n_prompt_tiles, tab_tiles, qn, kvn, ikn, wuq, wk):
    n = proj.shape[0]

    def tok(w):
        return pl.BlockSpec((TM_POST, w), lambda i: (i, 0))

    def full(a):
        return pl.BlockSpec(a.shape, lambda i: (0,) * a.ndim)

    def tab_map(i):
        return (jnp.where(i < n_prompt_tiles, i % tab_tiles, tab_tiles), 0)

    widths_f32 = [ROW_A, ROW_B, ROW_C, ROW_D, 512, 512, LANES, 512, 256, LANES, 512]
    widths_b16 = [LANES, 2 * LANES, 2 * LANES, LANES]
    out_shape = ([jax.ShapeDtypeStruct((n, w), F32) for w in widths_f32]
                 + [jax.ShapeDtypeStruct((n, w), BF16) for w in widths_b16])
    return pl.pallas_call(
        _post_kernel,
        out_shape=out_shape,
        grid=(n // TM_POST,),
        in_specs=[tok(IN_COLS_P), pl.BlockSpec((TM_POST, N_TAB * LANES), tab_map),
                  full(qn), full(kvn), full(ikn), full(wuq), full(wk)],
        out_specs=[tok(w) for w in widths_f32 + widths_b16],
        compiler_params=_cparams(("arbitrary",)),
        name="split_rope",
    )(proj, tab, qn, kvn, ikn, wuq, wk)


def _softmax_pv(q16, pieces, scale, keys_on_lanes=False):
    ss = []
    m = None
    r = q16.shape[0]
    for kv16, bias in pieces:
        s = (_dot(q16, kv16) if keys_on_lanes else _dot_nt(q16, kv16)) * scale
        if bias is not None and bias.shape[0] != r:
            t, lk = bias.shape
            s = (s.reshape(r // t, t, lk) + bias[None]).reshape(r, lk)
        elif bias is not None:
            s = s + bias
        ss.append(s)
        pm = jnp.max(s, axis=-1, keepdims=True)
        m = pm if m is None else jnp.maximum(m, pm)
    l = None
    pv = None
    for s, (kv16, _) in zip(ss, pieces):
        e = jnp.exp(s - m)
        pl_ = jnp.sum(e, axis=-1, keepdims=True)
        pp = _dot_nt(e.astype(BF16), kv16) if keys_on_lanes else _dot(e.astype(BF16), kv16)
        l = pl_ if l is None else l + pl_
        pv = pp if pv is None else pv + pp
    return pv, l


def _take_head(x_ref, col, width, dst, t):
    g, o = divmod(col, LANES)
    x = x_ref[:, g * LANES:(g + 1) * LANES]
    shift = (dst - o) % LANES
    if shift:
        x = pltpu.roll(x, shift, 1)
    lane = _lane((t, LANES))
    return jnp.where((lane >= dst) & (lane < dst + width), x, 0.0)


def _stack_bias(bias, reps):
    return bias if reps == 1 else jnp.concatenate([bias] * reps, axis=0)


def _mixer_a(qa_ref, heads, t, pieces_fn, lam, lam_init, subln):
    q1 = [_take_head(qa_ref, h * 2 * D_A, D_A, 0, t) for h in heads]
    q2 = [_take_head(qa_ref, h * 2 * D_A + D_A, D_A, D_A, t) for h in heads]
    q16 = jnp.concatenate(q1 + q2, axis=0).astype(BF16)
    n = len(heads) * t
    pv, l = _softmax_pv(q16, pieces_fn(2 * len(heads)), D_A ** -0.5)
    o = pv[:n] / l[:n] - lam * (pv[n:] / l[n:])
    lane = _lane((n, LANES))
    o = jnp.where(lane >= 2 * D_A, o, 0.0)
    ms = jnp.sum(o * o, axis=-1, keepdims=True) * (1.0 / (2 * D_A))
    y = o * lax.rsqrt(ms + EPS) * subln * (1.0 - lam_init)
    return [y[i * t:(i + 1) * t] for i in range(len(heads))]


def _mixer_b(ql_ref, qr_ref, heads, t, pieces_fn, wv_ref, keys_on_lanes=False):
    qs = []
    for h in heads:
        qs.append(jnp.concatenate([ql_ref[:, h * KV_LORA:(h + 1) * KV_LORA],
                                   _take_head(qr_ref, h * ROPE_B, ROPE_B, 0, t)], axis=1))
    q16 = jnp.concatenate(qs, axis=0).astype(BF16)
    pv, l = _softmax_pv(q16, pieces_fn(len(heads)), (NOPE_B + ROPE_B) ** -0.5, keys_on_lanes)
    o_lat = (pv[:, :KV_LORA] / l).astype(BF16)
    return [_dot(o_lat[i * t:(i + 1) * t], wv_ref[h]) for i, h in enumerate(heads)]


def _indexer_scores(iq_ref, iw_ref, t, kv_list, keys_on_lanes=False):
    zero = jnp.zeros((t, LANES), F32)
    q16 = jnp.concatenate(
        [jnp.concatenate([zero, _take_head(iq_ref, h * D_IDX, D_IDX, 0, t)], axis=1) for h in range(H_IDX)],
        axis=0).astype(BF16)
    scores = []
    for kv16 in kv_list:
        raw = _dot(q16, kv16) if keys_on_lanes else _dot_nt(q16, kv16)
        rel = jnp.maximum(raw * (D_IDX ** -0.5), 0.0)
        score = None
        for h in range(H_IDX):
            part = rel[h * t:(h + 1) * t] * iw_ref[:, h:h + 1]
            score = part if score is None else score + part
        scores.append(score)
    return scores


def _topk_mask(score, valid, n_keep):
    t, ln = score.shape
    bits = pltpu.bitcast(score + 0.0, jnp.int32)
    key = bits ^ ((bits >> 31) & 0x7FFFFFFF)
    key = jnp.where(valid, key, INT_MIN)

    def count(m):
        return jnp.sum(jnp.where(m, 1.0, 0.0), axis=-1, keepdims=True)

    step_bits = 2
    thr = jnp.full((t, 1), INT_MIN, jnp.int32)
    for bit in range(32 - step_bits, -1, -step_bits):
        cands = [thr + (((j << bit) + 2 ** 31) % 2 ** 32 - 2 ** 31) for j in range(1, 1 << step_bits)]
        oks = [count(key >= c) >= n_keep for c in cands]
        for c, ok in zip(cands, oks):
            thr = jnp.where(ok, c, thr)
    gt = key > thr
    eq = key == thr
    n_gt = count(gt)
    need = n_keep - n_gt
    col = _lane((t, ln))

    def split_ties():
        last = jnp.zeros((t, 1), jnp.int32)
        for bit in range(int(ln - 1).bit_length() - 1, -1, -1):
            cand = last + (1 << bit)
            last = jnp.where(count(eq & (col < cand)) < need, cand, last)
        return last

    surplus = jnp.where((count(eq) > need) & (thr > INT_MIN), 1.0, 0.0)
    last = lax.cond(jnp.max(surplus) > 0.0, split_ties, lambda: jnp.full((t, 1), ln, jnp.int32))
    return valid & (gt | (eq & (col <= last)))


def _mixer_c(qc_ref, heads, t, pieces_fn, keys_on_lanes=False):
    qs = [jnp.concatenate([_take_head(qc_ref, h * D_C, D_C, 0, t), jnp.zeros((t, LANES), F32)], axis=1)
          for h in heads]
    q16 = jnp.concatenate(qs, axis=0).astype(BF16)
    pv, l = _softmax_pv(q16, pieces_fn(len(heads)), D_C ** -0.5, keys_on_lanes)
    o = pv[:, :LANES] / l
    lane = _lane(o.shape)
    o = jnp.where(lane >= D_C, o, 0.0)
    return [o[i * t:(i + 1) * t] for i in range(len(heads))]


def _moba_select(qd_ref, heads, t, kmean16, own):
    qs = [_take_head(qd_ref, h * D_D, D_D, 0, t) for h in heads]
    q = jnp.concatenate(qs, axis=0)
    gate = _dot_nt(q.astype(BF16), kmean16)
    r = gate.shape[0]
    lane = _lane((r, LANES))
    ok = lane < own
    gate = jnp.where(ok, gate, NEG_INF)
    lane_f = lane.astype(F32)
    sel = jnp.zeros((r, LANES), F32)
    for _ in range(MOBA_TOPK):
        m = jnp.max(gate, axis=-1, keepdims=True)
        first = jnp.min(jnp.where(gate == m, lane_f, float(LANES)), axis=-1, keepdims=True)
        hit = lane_f == first
        sel = jnp.where(hit, 1.0, sel)
        gate = jnp.where(hit, NEG_INF, gate)
    return q, jnp.where(ok, sel, 0.0).astype(BF16)


def _mixer_d(q, n_heads, t, pieces):
    pv, l = _softmax_pv(q.astype(BF16), pieces, D_D ** -0.5)
    o = pv / l
    lane = _lane(o.shape)
    o = jnp.where(lane >= D_D, o, 0.0)
    return [o[i * t:(i + 1) * t] for i in range(n_heads)]


def _pair_store(o_ref, ys):
    for g in range(len(ys) // 2):
        o_ref[:, g * LANES:(g + 1) * LANES] = (pltpu.roll(ys[2 * g], D_C, 1) + ys[2 * g + 1]).astype(o_ref.dtype)


def _bias(mask):
    return jnp.where(mask, 0.0, NEG_INF)


def _lam(lv_ref, lam_init):
    lv = lv_ref[...]
    a = jnp.sum(lv[0:1] * lv[1:2], axis=-1, keepdims=True)
    b = jnp.sum(lv[2:3] * lv[3:4], axis=-1, keepdims=True)
    return jnp.exp(a) - jnp.exp(b) + lam_init


def _causal_bias(lk):
    q0 = pl.program_id(1) * TQ
    return _bias(_lane((TQ, lk)) <= q0 + _row((TQ, lk)))


def _for_key_prefix(seq, body):
    i = pl.program_id(1)
    per = seq // TQ // KEY_BUCKETS
    for j in range(KEY_BUCKETS):
        @pl.when((i >= j * per) & (i < (j + 1) * per))
        def _(j=j):
            body((j + 1) * per * TQ)


def _prompt_a_kernel(lam_init, qa_ref, k_ref, lv_ref, g_ref, o_ref):
    def body(lk):
        cb = _causal_bias(lk)
        lam = _lam(lv_ref, lam_init)
        kv = k_ref[:lk]
        ys = []
        for h0 in range(0, H_A, HEADS_PER_PASS_A):
            ys += _mixer_a(qa_ref, list(range(h0, h0 + HEADS_PER_PASS_A)), TQ, lambda reps: [(kv, cb)],
                           lam, lam_init, g_ref[...])
        _pair_store(o_ref, ys)

    _for_key_prefix(k_ref.shape[0], body)


def _prompt_b_kernel(ql_ref, qr_ref, k_ref, wv_ref, o_ref):
    def body(lk):
        cb = _causal_bias(lk)
        kv = k_ref[:lk]
        ys = _mixer_b(ql_ref, qr_ref, list(range(H_B)), TQ, lambda reps: [(kv, cb)], wv_ref)
        for h, y in enumerate(ys):
            o_ref[:, h * V_B:(h + 1) * V_B] = y.astype(o_ref.dtype)

    _for_key_prefix(k_ref.shape[0], body)


def _prompt_c_kernel(qc_ref, iq_ref, iw_ref, k_ref, o_ref):
    seq = k_ref.shape[0]
    n_keep = min(DSA_TOPK, seq // 4)

    def body(lk):
        assert lk >= n_keep
        q0 = pl.program_id(1) * TQ
        visible = _lane((TQ, lk)) <= q0 + _row((TQ, lk))
        kv = k_ref[:lk]
        (score,) = _indexer_scores(iq_ref, iw_ref, TQ, [kv])
        sb = _bias(_topk_mask(score, visible, n_keep))
        ys = []
        for h0 in range(0, H_C, HEADS_PER_PASS_C):
            ys += _mixer_c(qc_ref, list(range(h0, h0 + HEADS_PER_PASS_C)), TQ, lambda reps: [(kv, sb)])
        _pair_store(o_ref, ys)

    _for_key_prefix(seq, body)


def _block_means(rows, n_blocks):
    w = rows.shape[1]
    km = jnp.sum(rows.reshape(n_blocks, MOBA_BLOCK, w), axis=1) * (1.0 / MOBA_BLOCK)
    return jnp.concatenate([km, jnp.zeros((LANES - n_blocks, w), F32)], axis=0)


def _prompt_d_kernel(qd_ref, k_ref, kf_ref, e_ref, o_ref, km_ref):
    lk = k_ref.shape[0]
    i = pl.program_id(1)

    @pl.when(i == 0)
    def _():
        km_ref[...] = _block_means(kf_ref[...], lk // MOBA_BLOCK).astype(BF16)

    def body(lk):
        q0 = i * TQ
        col = _lane((TQ, lk))
        qpos = q0 + _row((TQ, lk))
        own_mask = ((col >> MOBA_SHIFT) == (qpos >> MOBA_SHIFT)) & (col <= qpos)
        own_bias = _bias(own_mask)
        kv = k_ref[:lk]
        n, r = HEADS_PER_PASS_D, HEADS_PER_PASS_D * TQ
        own = (q0 + (_row((r, 1)) & (TQ - 1))) >> MOBA_SHIFT
        ys = []
        for h0 in range(0, H_D, n):
            q, sel = _moba_select(qd_ref, list(range(h0, h0 + n)), TQ, km_ref[...], own)
            picked = _bias(_dot(sel, e_ref[:, :lk]) > 0.5)
            bias = jnp.maximum(picked.reshape(n, TQ, lk), own_bias[None]).reshape(r, lk)
            ys += _mixer_d(q, n, TQ, [(kv, bias)])
        _pair_store(o_ref, ys)

    _for_key_prefix(lk, body)


def _prompt_call(kernel, n_batch, seq, q_arrays, kv_arrays, consts, name, scratch=()):
    nq = seq // TQ

    def qspec(a):
        return pl.BlockSpec((TQ, a.shape[1]), lambda b, i: (b * nq + i, 0))

    def kspec(a):
        return pl.BlockSpec((seq, a.shape[1]), lambda b, i: (b, 0))

    def cspec(a):
        return pl.BlockSpec(a.shape, lambda b, i: (0,) * a.ndim)

    return pl.pallas_call(
        kernel,
        out_shape=jax.ShapeDtypeStruct((n_batch * seq, GROUP_W), BF16),
        grid=(n_batch, nq),
        in_specs=[qspec(a) for a in q_arrays] + [kspec(a) for a in kv_arrays] + [cspec(a) for a in consts],
        out_specs=pl.BlockSpec((TQ, GROUP_W), lambda b, i: (b * nq + i, 0)),
        scratch_shapes=list(scratch),
        compiler_params=_cparams(("arbitrary", "arbitrary")),
        name=name,
    )(*q_arrays, *kv_arrays, *consts)


def _paged_fetch(layer, pt_ref, cache_ref, buf_ref, sem_ref, n_pages, page, keys_on_lanes=False):
    b = pl.program_id(0)

    def copies(seq, slot):
        def dst(p):
            if keys_on_lanes:
                return buf_ref.at[slot, :, pl.ds(p * page, page)]
            return buf_ref.at[slot, pl.ds(p * page, page)]

        return [pltpu.make_async_copy(cache_ref.at[layer, pt_ref[seq, p]], dst(p), sem_ref.at[slot])
                for p in range(n_pages)]

    @pl.when(b == 0)
    def _():
        for c in copies(0, 0):
            c.start()

    @pl.when(b + 1 < pl.num_programs(0))
    def _():
        for c in copies(b + 1, (b + 1) % 2):
            c.start()

    slot = b % 2
    for c in copies(b, slot):
        c.wait()
    return slot


def _fill_keys(kv_ref, buf_ref, slot, new_ref, past, width):
    chunk = 512

    def body(c, carry):
        r = pl.multiple_of(c * chunk, chunk)
        kv_ref[pl.ds(r, chunk), :width] = buf_ref[slot, pl.ds(r, chunk), :].astype(BF16)
        return carry

    lax.fori_loop(0, past // chunk, body, 0)
    t = new_ref.shape[0]
    kv_ref[past:past + t, :width] = new_ref[...].astype(BF16)


def _fill_keys_t(kv_ref, buf_ref, slot, newt_ref, past, width):
    chunk = min(past, 1024)

    def body(c, carry):
        r = pl.multiple_of(c * chunk, chunk)
        kv_ref[:width, pl.ds(r, chunk)] = buf_ref[slot, :, pl.ds(r, chunk)].astype(BF16)
        return carry

    lax.fori_loop(0, past // chunk, body, 0)
    kv_ref[:width, past:] = newt_ref[0].astype(BF16)


def _new_bias(t, reps):
    r = reps * t
    return _bias(_lane((r, LANES)) <= _row((r, LANES)) % t)


def _zero_tail(kv_ref, past, t):
    @pl.when(pl.program_id(0) == 0)
    def _():
        kv_ref[...] = jnp.zeros(kv_ref.shape, kv_ref.dtype)


def _sample_a_kernel(layer, lam_init, n_pages, page, pt_ref, qa_ref, new_ref, lv_ref, g_ref, cache_ref,
                     o_ref, buf_ref, sem_ref, kv_ref):
    past = n_pages * page
    t = qa_ref.shape[0]
    _zero_tail(kv_ref, past, t)
    slot = _paged_fetch(layer, pt_ref, cache_ref, buf_ref, sem_ref, n_pages, page)
    _fill_keys(kv_ref, buf_ref, slot, new_ref, past, ROW_A)
    lam = _lam(lv_ref, lam_init)

    def pieces(reps):
        return [(kv_ref[:past], None), (kv_ref[past:], _new_bias(t, reps))]

    ys = _mixer_a(qa_ref, list(range(H_A)), t, pieces, lam, lam_init, g_ref[...])
    _pair_store(o_ref, ys)


def _sample_b_kernel(layer, n_pages, page, pt_ref, ql_ref, qr_ref, new_ref, wv_ref, cache_ref,
                     o_ref, buf_ref, sem_ref, kv_ref):
    past = n_pages * page
    t = ql_ref.shape[0]
    _zero_tail(kv_ref, past, t)
    slot = _paged_fetch(layer, pt_ref, cache_ref, buf_ref, sem_ref, n_pages, page, keys_on_lanes=True)
    _fill_keys_t(kv_ref, buf_ref, slot, new_ref, past, ROW_B)

    def pieces(reps):
        return [(kv_ref[:, :past], None), (kv_ref[:, past:], _new_bias(t, reps))]

    ys = _mixer_b(ql_ref, qr_ref, list(range(H_B)), t, pieces, wv_ref, keys_on_lanes=True)
    for h, y in enumerate(ys):
        o_ref[:, h * V_B:(h + 1) * V_B] = y.astype(o_ref.dtype)


def _sample_c_kernel(layer, n_pages, page, pt_ref, qc_ref, iq_ref, iw_ref, new_ref, cache_ref,
                     o_ref, buf_ref, sem_ref, kv_ref):
    past = n_pages * page
    t = qc_ref.shape[0]
    _zero_tail(kv_ref, past, t)
    slot = _paged_fetch(layer, pt_ref, cache_ref, buf_ref, sem_ref, n_pages, page, keys_on_lanes=True)
    _fill_keys_t(kv_ref, buf_ref, slot, new_ref, past, ROW_C)
    s_past, s_new = _indexer_scores(iq_ref, iw_ref, t, [kv_ref[:, :past], kv_ref[:, past:]], keys_on_lanes=True)
    score = jnp.concatenate([s_past, s_new], axis=1)
    col = _lane(score.shape)
    visible = col <= past + _row(score.shape)
    sb = _bias(_topk_mask(score, visible, min(DSA_TOPK, (past + t) // 4)))

    def pieces(reps):
        return [(kv_ref[:, :past], _stack_bias(sb[:, :past], reps)),
                (kv_ref[:, past:], _stack_bias(sb[:, past:], reps))]

    ys = _mixer_c(qc_ref, list(range(H_C)), t, pieces, keys_on_lanes=True)
    _pair_store(o_ref, ys)


def _sample_d_kernel(layer, n_pages, page, pt_ref, qd_ref, new_ref, e_ref, cache_ref,
                     o_ref, buf_ref, sem_ref, kv_ref):
    past = n_pages * page
    t = qd_ref.shape[0]
    _zero_tail(kv_ref, past, t)
    slot = _paged_fetch(layer, pt_ref, cache_ref, buf_ref, sem_ref, n_pages, page)
    _fill_keys(kv_ref, buf_ref, slot, new_ref, past, ROW_D)
    km16 = _block_means(buf_ref[slot], past // MOBA_BLOCK).astype(BF16)
    own = jnp.full((H_D * t, 1), past // MOBA_BLOCK, jnp.int32)
    q, sel = _moba_select(qd_ref, list(range(H_D)), t, km16, own)
    allowed = _dot(sel, e_ref[...]) > 0.5
    ys = _mixer_d(q, H_D, t, [(kv_ref[:past], _bias(allowed)), (kv_ref[past:], _new_bias(t, H_D))])
    _pair_store(o_ref, ys)


def _sample_call(kernel, page_table, cache, tok_off, t, q_arrays, new_rows, consts, kv_width, name,
                 keys_on_lanes=False):
    n_seq, n_pages = page_table.shape
    if keys_on_lanes:
        width, page = cache.shape[2], cache.shape[3]
    else:
        page, width = cache.shape[2], cache.shape[3]
    past = n_pages * page
    off = tok_off // t

    def tspec(a):
        return pl.BlockSpec((t, a.shape[1]), lambda b, pt: (b + off, 0))

    def cspec(a):
        return pl.BlockSpec(a.shape, lambda b, pt: (0,) * a.ndim)

    if keys_on_lanes:
        new_spec = pl.BlockSpec((1, width, LANES), lambda b, pt: (b, 0, 0))
        buf_shape, kv_shape = (2, width, past), (kv_width, past + LANES)
    else:
        new_spec = tspec(new_rows)
        buf_shape, kv_shape = (2, past, width), (past + LANES, kv_width)

    grid_spec = pltpu.PrefetchScalarGridSpec(
        num_scalar_prefetch=1,
        grid=(n_seq,),
        in_specs=([tspec(a) for a in q_arrays] + [new_spec] + [cspec(a) for a in consts]
                  + [pl.BlockSpec(memory_space=pl.ANY)]),
        out_specs=pl.BlockSpec((t, GROUP_W), lambda b, pt: (b, 0)),
        scratch_shapes=[pltpu.VMEM(buf_shape, F32),
                        pltpu.SemaphoreType.DMA((2,)),
                        pltpu.VMEM(kv_shape, BF16)],
    )
    return pl.pallas_call(
        functools.partial(kernel, n_pages, page),
        out_shape=jax.ShapeDtypeStruct((n_seq * t, GROUP_W), F32),
        grid_spec=grid_spec,
        compiler_params=_cparams(("arbitrary",)),
        name=name,
    )(page_table, *q_arrays, new_rows, *consts, cache)


def _rope_tables(pos):
    lane = np.arange(LANES)

    def pattern(dh, n_rot, lo, hi):
        half = n_rot // 2
        inv = jnp.exp(-math.log(ROPE_THETA) * jnp.arange(half, dtype=F32) / half)
        ang = pos.astype(F32)[:, None] * inv
        cos, sin = jnp.cos(ang), jnp.sin(ang)
        r = lane % dh
        act = (lane >= lo) & (lane < hi)
        idx = r % half
        c = jnp.where((r < n_rot) & act, cos[:, idx], 1.0)
        s1 = jnp.where((r < half) & act, -sin[:, idx], 0.0)
        s2 = jnp.where((r >= half) & (r < n_rot) & act, sin[:, idx], 0.0)
        return c, s1, s2

    a = pattern(D_A, ROT_A, 0, LANES)
    c = pattern(D_C, ROT_C, 0, LANES)
    ra = pattern(D_A, ROT_A, 0, 2 * D_A)
    rc = pattern(D_C, ROT_C, 0, D_C)
    b = pattern(ROPE_B, ROPE_B, 0, LANES)
    mb = pattern(ROPE_B, ROPE_B, 0, ROPE_B)
    mi = pattern(D_IDX, ROT_IDX, ROPE_B, ROPE_B + D_IDX)
    misc = (mb[0] * mi[0], mb[1], mb[2], mi[1], mi[2])
    return jnp.concatenate(list(a + c + ra + rc + b + misc), axis=1)


def _prep_layer_weights(w_in, w_out, mla_w_uq, mla_w_ukv, w_up, w_down):
    cuts = np.cumsum([0, H_A * 2 * D_A, 2 * D_A, 2 * D_A, Q_LORA, KV_LORA, ROPE_B,
                      H_C * D_C, D_C, D_C, H_IDX * D_IDX, D_IDX, H_IDX, H_D * D_D, D_D, D_D])
    names = ["qa", "ka", "va", "cq", "ckv", "kr", "qc", "kc", "vc", "iq", "ik", "iw", "qd", "kd", "vd"]
    seg = {n: w_in[..., int(cuts[i]):int(cuts[i + 1])] for i, n in enumerate(names)}
    pad = jnp.zeros(w_in.shape[:-1] + (LANES - ROPE_B - D_IDX - H_IDX,), w_in.dtype)
    order = ["qa", "qc", "qd", "iq", "cq", "ka", "va", "ckv", "kr", "ik", "iw", None, "kc", "vc", "kd", "vd"]
    w_in_p = jnp.concatenate([pad if n is None else seg[n] for n in order], axis=-1).astype(BF16)
    depth = w_in.shape[0]
    uq = mla_w_uq.reshape(depth, Q_LORA, H_B, NOPE_B + ROPE_B)
    w_uq_p = jnp.concatenate([uq[..., :NOPE_B].reshape(depth, Q_LORA, H_B * NOPE_B),
                              uq[..., NOPE_B:].reshape(depth, Q_LORA, H_B * ROPE_B)], axis=-1).astype(BF16)
    wk = jnp.transpose(mla_w_ukv[..., :NOPE_B], (0, 2, 3, 1))
    eye = jnp.eye(H_B, dtype=wk.dtype)
    wk_bd = jnp.einsum("lhnc,hg->lhngc", wk, eye).reshape(depth, H_B * NOPE_B, H_B * KV_LORA).astype(BF16)
    wv = jnp.transpose(mla_w_ukv[..., NOPE_B:], (0, 2, 1, 3)).astype(BF16)
    return w_in_p, w_out.astype(BF16), w_uq_p, wk_bd, wv, w_up.astype(BF16), w_down.astype(BF16)


def kernel(x_prompt, x_sample, cache_diff, cache_mla, cache_dsa, cache_moba, page_table, norm_mix, norm_mlp,
           norm_final, w_in, w_out, diff_lambda, diff_subln, mla_q_norm, mla_kv_norm, mla_w_uq, mla_w_ukv,
           dsa_idx_k_norm, w_up, w_down):
    n_b, seq, d = x_prompt.shape
    n_seq, t_dec, _ = x_sample.shape
    depth = w_in.shape[0]
    n_pages = page_table.shape[1]
    page = cache_diff.shape[2]
    past = n_pages * page
    n_p = n_b * seq
    n_s = n_seq * t_dec
    assert n_p % TM_PROJ == 0 and n_s % TM_PROJ == 0 and seq % TM_POST == 0 and n_s % TM_POST == 0
    assert seq % MOBA_BLOCK == 0 and past % MOBA_BLOCK == 0 and TM_POST % t_dec == 0
    assert seq % (TQ * KEY_BUCKETS) == 0

    w_in_p, w_out16, w_uq_p, wk_bd, wv, w_up16, w_down16 = _prep_layer_weights(
        w_in, w_out, mla_w_uq, mla_w_ukv, w_up, w_down)
    pos_tab = jnp.concatenate([jnp.arange(seq), past + jnp.arange(TM_POST) % t_dec])
    tab = _rope_tables(pos_tab)
    lane = np.arange(LANES)
    e_prompt = jnp.asarray((np.arange(seq)[None, :] // MOBA_BLOCK) == lane[:, None], BF16)
    e_sample = jnp.asarray((np.arange(past)[None, :] // MOBA_BLOCK) == lane[:, None], BF16)

    cache_mla_t = jnp.swapaxes(cache_mla, 2, 3)
    cache_dsa_t = jnp.swapaxes(cache_dsa, 2, 3)

    def new_t(rows):
        r = jnp.swapaxes(rows[n_p:].reshape(n_seq, t_dec, rows.shape[1]), 1, 2)
        return jnp.pad(r, ((0, 0), (0, 0), (0, LANES - t_dec)))

    h = jnp.concatenate([x_prompt.reshape(n_p, d), x_sample.reshape(n_s, d)], axis=0)
    rows_p = [[], [], [], []]
    rows_s = [[], [], [], []]
    for l in range(depth):
        lam_init = 0.8 - 0.6 * math.exp(-0.3 * l)
        proj = _norm_matmul(h, norm_mix[l][None], w_in_p[l])
        ikn = jnp.zeros((1, LANES), F32).at[0, ROPE_B:ROPE_B + D_IDX].set(dsa_idx_k_norm[l])
        (ra, rb, rc, rd, qa, ql, qr, qc, iq, iw, qd, ka16, kb16, kc16, kd16) = _post(
            proj, tab, n_p // TM_POST, seq // TM_POST, mla_q_norm[l][None], mla_kv_norm[l][None], ikn,
            w_uq_p[l], wk_bd[l])
        subln = jnp.zeros((1, LANES), F32).at[0, 2 * D_A:].set(diff_subln[l])
        lv = diff_lambda[l]

        mp = [
            _prompt_call(functools.partial(_prompt_a_kernel, lam_init), n_b, seq, [qa], [ka16], [lv, subln],
                         "prompt_diff"),
            _prompt_call(_prompt_b_kernel, n_b, seq, [ql, qr], [kb16], [wv[l]], "prompt_mla"),
            _prompt_call(_prompt_c_kernel, n_b, seq, [qc, iq, iw], [kc16], [], "prompt_dsa"),
            _prompt_call(_prompt_d_kernel, n_b, seq, [qd], [kd16, rd], [e_prompt], "prompt_moba",
                         scratch=[pltpu.VMEM((LANES, LANES), BF16)]),
        ]
        ms = [
            _sample_call(functools.partial(_sample_a_kernel, l, lam_init), page_table, cache_diff, n_p, t_dec,
                         [qa], ra, [lv, subln], LANES, "sample_diff"),
            _sample_call(functools.partial(_sample_b_kernel, l), page_table, cache_mla_t, n_p, t_dec,
                         [ql, qr], new_t(rb), [wv[l]], 2 * LANES, "sample_mla", keys_on_lanes=True),
            _sample_call(functools.partial(_sample_c_kernel, l), page_table, cache_dsa_t, n_p, t_dec,
                         [qc, iq, iw], new_t(rc), [], 2 * LANES, "sample_dsa", keys_on_lanes=True),
            _sample_call(functools.partial(_sample_d_kernel, l), page_table, cache_moba, n_p, t_dec,
                         [qd], rd, [e_sample], LANES, "sample_moba"),
        ]
        mixed = jnp.concatenate([jnp.concatenate(mp, axis=1),
                                 jnp.concatenate(ms, axis=1).astype(BF16)], axis=0)
        h = _matmul_res(mixed, w_out16[l], h)
        h = _mlp(h, norm_mlp[l][None], w_up16[l], w_down16[l])
        for j, r in enumerate((ra, rb, rc, rd)):
            rows_p[j].append(r[:n_p].reshape(n_b, seq, r.shape[1]))
            rows_s[j].append(r[n_p:].reshape(n_seq, t_dec, r.shape[1]))

    y = _final_norm(h, norm_final[None])
    outs = [y[:n_p].reshape(n_b, seq, d), y[n_p:].reshape(n_seq, t_dec, d)]
    for j in range(4):
        outs += [jnp.stack(rows_p[j]), jnp.stack(rows_s[j])]
    return tuple(outs)
```
